```python
import math
import jax, jax.numpy as jnp
from jax import lax
import numpy as np

D_MODEL = 1024
BATCH = 16
SEQ = 4096
DEPTH = 4
DEC_BATCH = 4
DEC_SEQ = 4096
PAST_LEN = 128

N_META = 16
BLOCK = 128
PAD = BLOCK - N_META
WINDOW = 128
HEAD_DIM_ATTN = 64
N_HEADS_ATTN = (D_MODEL // 2) // HEAD_DIM_ATTN
N_KV_HEADS = 2
ATTN_WIDTH = N_HEADS_ATTN * HEAD_DIM_ATTN
KV_WIDTH = N_KV_HEADS * HEAD_DIM_ATTN
NUM_BUCKETS = 32
MAX_DISTANCE = 128
MASK_VALUE = -1e30
REC_KEY_DIM = 128
N_HEADS_REC = 4
REC_VAL_DIM = (D_MODEL // 2) // N_HEADS_REC
REC_KEY_WIDTH = N_HEADS_REC * REC_KEY_DIM
REC_VAL_WIDTH = N_HEADS_REC * REC_VAL_DIM
CHUNK = 64
MIN_FORGET = 1e-30
MIX_WIDTH = ATTN_WIDTH + REC_VAL_WIDTH
D_FF = 11 * D_MODEL // 4
FFN_RESIDUAL = 0.5
N_NORMS = 6
EPS = 1e-6
IN_SIZES = (ATTN_WIDTH, KV_WIDTH, KV_WIDTH, REC_KEY_WIDTH, REC_KEY_WIDTH, REC_KEY_WIDTH, REC_VAL_WIDTH, REC_VAL_WIDTH)
IN_WIDTH = sum(IN_SIZES)
IN_SPLITS = [int(s) for s in np.cumsum(IN_SIZES)[:-1]]

kernel_name = "hymba_hgrn2_swa_macaron_encoder"


def rms_norm(x, g):
    xf = x.astype(jnp.float32)
    y = xf * lax.rsqrt(jnp.mean(xf * xf, axis=-1, keepdims=True) + EPS)
    return (y * g.astype(jnp.float32)).astype(x.dtype)


def swiglu(x, w_in, w_out):
    a, b = jnp.split(x @ w_in, 2, axis=-1)
    return (jax.nn.silu(a) * b) @ w_out


def t5_bucket(rel):
    half = NUM_BUCKETS // 2
    max_exact = half // 2
    ret = (rel > 0).astype(jnp.int32) * half
    n = jnp.abs(rel)
    nf = jnp.maximum(n, 1).astype(jnp.float32)
    large = max_exact + (jnp.log(nf / max_exact) / math.log(MAX_DISTANCE / max_exact) * (half - max_exact)).astype(jnp.int32)
    large = jnp.clip(large, 0, half - 1)
    return ret + jnp.where(n < max_exact, n, large)


def windowed_attention(q, k, v, sink, rel_bias):
    bsz, lp = q.shape[0], q.shape[1]
    n_blocks = lp // BLOCK
    groups = N_HEADS_ATTN // N_KV_HEADS
    scale = HEAD_DIM_ATTN ** -0.5
    meta_k = k[:, PAD:BLOCK]
    meta_v = v[:, PAD:BLOCK]
    k_pad = jnp.pad(k, ((0, 0), (BLOCK, BLOCK), (0, 0), (0, 0)))
    v_pad = jnp.pad(v, ((0, 0), (BLOCK, BLOCK), (0, 0), (0, 0)))
    meta_pos = jnp.arange(PAD, BLOCK)
    is_meta = jnp.arange(N_META + 3 * BLOCK) < N_META
    sink_col = sink.astype(jnp.float32).reshape(1, N_KV_HEADS, groups, 1, 1)
    table = rel_bias.astype(jnp.float32)

    def one_block(b):
        start = b * BLOCK
        qb = lax.dynamic_slice_in_dim(q, start, BLOCK, axis=1).reshape(bsz, BLOCK, N_KV_HEADS, groups, HEAD_DIM_ATTN)
        kb = jnp.concatenate([meta_k, lax.dynamic_slice_in_dim(k_pad, start, 3 * BLOCK, axis=1)], axis=1)
        vb = jnp.concatenate([meta_v, lax.dynamic_slice_in_dim(v_pad, start, 3 * BLOCK, axis=1)], axis=1)
        q_pos = start + jnp.arange(BLOCK)
        band_pos = start - BLOCK + jnp.arange(3 * BLOCK)
        k_pos = jnp.concatenate([meta_pos, band_pos])
        rel = k_pos[None, :] - q_pos[:, None]
        band_ok = (band_pos >= BLOCK) & (band_pos < lp)
        key_ok = jnp.concatenate([jnp.ones((N_META,), bool), band_ok])
        valid = key_ok[None, :] & (is_meta[None, :] | (jnp.abs(rel) <= WINDOW))
        bias = jnp.transpose(table[t5_bucket(rel)], (2, 0, 1)).reshape(N_KV_HEADS, groups, BLOCK, -1)
        logits = jnp.einsum('bqkgd,bskd->bkgqs', qb, kb).astype(jnp.float32) * scale + bias
        logits = jnp.where(valid, logits, MASK_VALUE)
        m = jnp.maximum(jnp.max(logits, axis=-1, keepdims=True), sink_col)
        e = jnp.exp(logits - m)
        denom = jnp.sum(e, axis=-1, keepdims=True) + jnp.exp(sink_col - m)
        probs = e / denom
        out = jnp.einsum('bkgqs,bskd->bqkgd', probs.astype(v.dtype), vb)
        return out.reshape(bsz, BLOCK, ATTN_WIDTH)

    out = lax.map(one_block, jnp.arange(n_blocks))
    return jnp.transpose(out, (1, 0, 2, 3)).reshape(bsz, lp, ATTN_WIDTH)


def hgrn2_chunk_scan(q, k, log_f, v):
    bsz, L, H, dk = q.shape
    dv = v.shape[-1]
    n = L // CHUNK

    def to_chunks(t):
        return t.reshape(bsz, n, CHUNK, H, t.shape[-1]).transpose(1, 0, 3, 2, 4)

    causal = jnp.tril(jnp.ones((CHUNK, CHUNK), bool))[:, :, None]

    def step(state, xs):
        qc, kc, gc, vc = xs
        G = jnp.cumsum(gc, axis=2)
        g_last = G[:, :, -1:, :]
        inter = jnp.einsum('bhtd,bhde->bhte', qc * jnp.exp(G), state)
        diff = G[:, :, :, None, :] - G[:, :, None, :, :]
        decay = jnp.where(causal, jnp.exp(jnp.minimum(diff, 0.0)), 0.0)
        scores = jnp.einsum('bhtd,bhsd,bhtsd->bhts', qc, kc, decay)
        intra = jnp.einsum('bhts,bhse->bhte', scores, vc)
        new_state = jnp.exp(g_last[:, :, 0, :])[..., None] * state + jnp.einsum('bhsd,bhse->bhde', kc * jnp.exp(g_last - G), vc)
        return new_state, inter + intra

    s0 = jnp.zeros((bsz, H, dk, dv), jnp.float32)
    _, out = lax.scan(step, s0, (to_chunks(q), to_chunks(k), to_chunks(log_f), to_chunks(v)))
    return out.transpose(1, 0, 3, 2, 4).reshape(bsz, L, H, dv)


def hgrn2_mixer(q, z_fwd, z_bwd, inp, gate, lb_fwd, lb_bwd, out_gain, valid):
    bsz, lp = q.shape[0], q.shape[1]

    def heads(t, d):
        return t.astype(jnp.float32).reshape(bsz, lp, N_HEADS_REC, d)

    qh = heads(q, REC_KEY_DIM)
    vh = heads(inp, REC_VAL_DIM)
    tok = valid[None, :, None, None]

    def gates(z, lb):
        zh = heads(z, REC_KEY_DIM)
        lbh = lb.reshape(N_HEADS_REC, REC_KEY_DIM)
        f = lbh + (1.0 - lbh) * jax.nn.sigmoid(zh)
        log_f = jnp.log(jnp.maximum(f, MIN_FORGET))
        kk = (1.0 - lbh) * jax.nn.sigmoid(-zh)
        return jnp.where(tok, log_f, 0.0), jnp.where(tok, kk, 0.0)

    lf_f, k_f = gates(z_fwd, lb_fwd)
    lf_b, k_b = gates(z_bwd, lb_bwd)
    o_f = hgrn2_chunk_scan(qh, k_f, lf_f, vh)
    flip = lambda t: jnp.flip(t, axis=1)
    o_b = flip(hgrn2_chunk_scan(flip(qh), flip(k_b), flip(lf_b), flip(vh)))
    o = o_f + o_b
    o = o * lax.rsqrt(jnp.mean(o * o, axis=-1, keepdims=True) + EPS)
    o = o.reshape(bsz, lp, REC_VAL_WIDTH) * out_gain.astype(jnp.float32) * jax.nn.silu(gate.astype(jnp.float32))
    return o.astype(q.dtype)


def encoder_trunk(x, meta_tokens, rel_bias, lower_bounds, norm_gains, w_in, attn_sink, attn_out_gain,
                  rec_out_gain, w_out, ffn1_w_in, ffn1_w_out, ffn2_w_in, ffn2_w_out):
    bsz, seq, _ = x.shape
    lp = PAD + N_META + seq
    h = jnp.concatenate([jnp.zeros((bsz, PAD, D_MODEL), x.dtype),
                         jnp.broadcast_to(meta_tokens.astype(x.dtype)[None], (bsz, N_META, D_MODEL)),
                         x], axis=1)
    valid = jnp.arange(lp) >= PAD
    for l in range(DEPTH):
        g = norm_gains[l]
        h = h + FFN_RESIDUAL * rms_norm(swiglu(rms_norm(h, g[0]), ffn1_w_in[l], ffn1_w_out[l]), g[1])
        u = rms_norm(h, g[2])
        qa, ka, va, qr, zf, zb, ir, gr = jnp.split(u @ w_in[l], IN_SPLITS, axis=-1)
        attn = windowed_attention(qa.reshape(bsz, lp, N_HEADS_ATTN, HEAD_DIM_ATTN),
                                  ka.reshape(bsz, lp, N_KV_HEADS, HEAD_DIM_ATTN),
                                  va.reshape(bsz, lp, N_KV_HEADS, HEAD_DIM_ATTN),
                                  attn_sink[l], rel_bias)
        attn = rms_norm(attn, attn_out_gain[l])
        rec = hgrn2_mixer(qr, zf, zb, ir, gr, lower_bounds[0, l], lower_bounds[1, l], rec_out_gain[l], valid)
        mix = jnp.concatenate([attn, rec], axis=-1) @ w_out[l]
        h = h + rms_norm(mix, g[3])
        h = h + FFN_RESIDUAL * rms_norm(swiglu(rms_norm(h, g[4]), ffn2_w_in[l], ffn2_w_out[l]), g[5])
    return h[:, PAD + N_META:]


def setup_inputs(seed: int = 0) -> dict:
    key = jax.random.key(seed)
    ks = jax.random.split(key, 16)
    nrm = lambda k, shape, s: jax.random.normal(k, shape, jnp.float32) * s
    return {
        'x_prompt': nrm(ks[0], (BATCH, SEQ, D_MODEL), 1.0),
        'x_sample': nrm(ks[1], (DEC_BATCH, DEC_SEQ, D_MODEL), 1.0),
        'meta_tokens': nrm(ks[2], (N_META, D_MODEL), 1.0),
        'rel_bias': nrm(ks[3], (NUM_BUCKETS, N_HEADS_ATTN), 0.5),
        'lb_logits': nrm(ks[4], (2, DEPTH, REC_KEY_WIDTH), 1.0),
        'norm_gains': 1.0 + nrm(ks[5], (DEPTH, N_NORMS, D_MODEL), 0.05),
        'w_in': nrm(ks[6], (DEPTH, D_MODEL, IN_WIDTH), D_MODEL ** -0.5),
        'attn_sink': nrm(ks[7], (DEPTH, N_HEADS_ATTN), 1.0),
        'attn_out_gain': 1.0 + nrm(ks[8], (DEPTH, ATTN_WIDTH), 0.05),
        'rec_out_gain': 1.0 + nrm(ks[9], (DEPTH, REC_VAL_WIDTH), 0.05),
        'w_out': nrm(ks[10], (DEPTH, MIX_WIDTH, D_MODEL), MIX_WIDTH ** -0.5),
        'ffn1_w_in': nrm(ks[11], (DEPTH, D_MODEL, 2 * D_FF), D_MODEL ** -0.5),
        'ffn1_w_out': nrm(ks[12], (DEPTH, D_FF, D_MODEL), D_FF ** -0.5),
        'ffn2_w_in': nrm(ks[13], (DEPTH, D_MODEL, 2 * D_FF), D_MODEL ** -0.5),
        'ffn2_w_out': nrm(ks[14], (DEPTH, D_FF, D_MODEL), D_FF ** -0.5),
    }


def reference(x_prompt, x_sample, meta_tokens, rel_bias, lb_logits, norm_gains, w_in, attn_sink, attn_out_gain,
              rec_out_gain, w_out, ffn1_w_in, ffn1_w_out, ffn2_w_in, ffn2_w_out):
    p = jax.nn.softmax(lb_logits.astype(jnp.float32), axis=1)
    lower_bounds = jnp.cumsum(p, axis=1) - p[:, :1]
    y_prompt = encoder_trunk(x_prompt, meta_tokens, rel_bias, lower_bounds, norm_gains, w_in, attn_sink,
                             attn_out_gain, rec_out_gain, w_out, ffn1_w_in, ffn1_w_out, ffn2_w_in, ffn2_w_out)
    y_sample = encoder_trunk(x_sample, meta_tokens, rel_bias, lower_bounds, norm_gains, w_in, attn_sink,
                             attn_out_gain, rec_out_gain, w_out, ffn1_w_in, ffn1_w_out, ffn2_w_in, ffn2_w_out)
    return (y_prompt, y_sample)
```

```python
import functools
import math

import numpy as np
import jax
import jax.numpy as jnp
from jax import lax
from jax.experimental import pallas as pl
from jax.experimental.pallas import tpu as pltpu

F32 = jnp.float32
BF16 = jnp.bfloat16

D_MODEL = 1024
N_META = 16
BLOCK = 128
PAD = BLOCK - N_META
WINDOW = 128
HEAD_DIM = 64
N_HEADS = 8
N_KV = 2
ATTN_W = N_HEADS * HEAD_DIM
KV_W = N_KV * HEAD_DIM
NUM_BUCKETS = 32
MAX_DISTANCE = 128
MASK_VALUE = -1e30
REC_HEADS = 4
REC_DK = 128
REC_DV = 128
REC_W = REC_HEADS * REC_DK
MIN_FORGET = 1e-30
D_FF = 2816
FFN_RESIDUAL = 0.5
EPS = 1e-6

FF_CHUNK = 256
N_FF_CHUNKS = D_FF // FF_CHUNK
CHUNK = 64
N_LEVELS = 6
IN_W = ATTN_W + 4 * KV_W + 5 * REC_W
VMEM_LIMIT = 52 * 1024 * 1024


def _rms(x, g):
    ms = jnp.mean(x * x, axis=-1, keepdims=True)
    return x * lax.rsqrt(ms + EPS) * g


def _pick_tm(seq_len):
    for tm in range(704, 15, -16):
        if seq_len % tm == 0:
            return tm
    raise ValueError(f"no token tile for sequence length {seq_len}")


def _ffn_kernel(x_ref, gpre_ref, gpost_ref, win_ref, wout_ref, o_ref, xn_ref, acc_ref):
    xn_ref[...] = _rms(x_ref[0], gpre_ref[...]).astype(BF16)
    acc_ref[...] = jnp.zeros_like(acc_ref)

    def body(c, carry):
        ab = jnp.dot(xn_ref[...], win_ref[c], preferred_element_type=F32)
        a = ab[:, :FF_CHUNK]
        b = ab[:, FF_CHUNK:]
        s = (a * jax.nn.sigmoid(a) * b).astype(BF16)
        acc_ref[...] += jnp.dot(s, wout_ref[c], preferred_element_type=F32)
        return carry

    lax.fori_loop(0, N_FF_CHUNKS, body, 0)
    o_ref[0] = x_ref[0] + FFN_RESIDUAL * _rms(acc_ref[...], gpost_ref[...])


def _ffn(h, gpre, gpost, win_c, wout_c):
    ns, L, _ = h.shape
    tm = _pick_tm(L)
    tok = pl.BlockSpec((1, tm, D_MODEL), lambda s, j: (s, j, 0))
    gain = pl.BlockSpec((1, D_MODEL), lambda s, j: (0, 0))
    return pl.pallas_call(
        _ffn_kernel,
        grid=(ns, L // tm),
        in_specs=[tok, gain, gain,
                  pl.BlockSpec((N_FF_CHUNKS, D_MODEL, 2 * FF_CHUNK), lambda s, j: (0, 0, 0),
                               pipeline_mode=pl.Buffered(1)),
                  pl.BlockSpec((N_FF_CHUNKS, FF_CHUNK, D_MODEL), lambda s, j: (0, 0, 0),
                               pipeline_mode=pl.Buffered(1))],
        out_specs=tok,
        out_shape=jax.ShapeDtypeStruct(h.shape, F32),
        scratch_shapes=[pltpu.VMEM((tm, D_MODEL), BF16), pltpu.VMEM((tm, D_MODEL), F32)],
        input_output_aliases={0: 0},
        compiler_params=pltpu.CompilerParams(
            dimension_semantics=("arbitrary", "arbitrary"), vmem_limit_bytes=VMEM_LIMIT),
        name="ffn",
    )(h, gpre, gpost, win_c, wout_c)


def _lower_bound(lbl, layer):
    m = jnp.max(lbl, axis=0, keepdims=True)
    e = jnp.exp(lbl - m)
    den = jnp.sum(e, axis=0, keepdims=True)
    if layer == 0:
        return jnp.zeros_like(den)
    return jnp.sum(e[1:layer + 1], axis=0, keepdims=True) / den


def _inproj_kernel(layer, tm, x_ref, g_ref, w_ref, lbl_ref,
                   q_ref, k_ref, v_ref, qr_ref, lff_ref, kf_ref, lfb_ref, kb_ref, ir_ref, sg_ref, u_ref):
    u_ref[...] = _rms(x_ref[0], g_ref[...]).astype(BF16)

    def proj(c0, n):
        return jnp.dot(u_ref[...], w_ref[:, c0:c0 + n], preferred_element_type=F32)

    q_ref[0] = proj(0, ATTN_W).astype(BF16)
    kv = proj(ATTN_W, 4 * KV_W)
    k_ref[0] = kv[:, :2 * KV_W].astype(BF16)
    v_ref[0] = kv[:, 2 * KV_W:].astype(BF16)
    c0 = ATTN_W + 4 * KV_W
    qr_ref[0] = proj(c0, REC_W).astype(BF16)

    rows = pl.program_id(1) * tm + lax.broadcasted_iota(jnp.int32, (tm, 1), 0)
    tok = rows >= PAD
    for d, (lf_ref, kk_ref) in enumerate(((lff_ref, kf_ref), (lfb_ref, kb_ref))):
        z = proj(c0 + (1 + d) * REC_W, REC_W)
        lb = _lower_bound(lbl_ref[d], layer)
        e = jnp.exp(-jnp.abs(z))
        r = 1.0 / (1.0 + e)
        er = e * r
        pos = z >= 0.0
        sig_p = jnp.where(pos, r, er)
        sig_n = jnp.where(pos, er, r)
        f = lb + (1.0 - lb) * sig_p
        lf_ref[0] = jnp.where(tok, jnp.log(jnp.maximum(f, MIN_FORGET)), 0.0)
        kk_ref[0] = jnp.where(tok, (1.0 - lb) * sig_n, 0.0)
    ir_ref[0] = proj(c0 + 3 * REC_W, REC_W).astype(BF16)
    gz = proj(c0 + 4 * REC_W, REC_W)
    sg_ref[0] = (gz * jax.nn.sigmoid(gz)).astype(BF16)


def _inproj(h, g, w, lb_logits, layer):
    ns, L, _ = h.shape
    tm = _pick_tm(L)
    depth = lb_logits.shape[1]

    def tok(width):
        return pl.BlockSpec((1, tm, width), lambda s, j: (s, j, 0))

    def out(width, dt):
        return jax.ShapeDtypeStruct((ns, L, width), dt)

    return pl.pallas_call(
        functools.partial(_inproj_kernel, layer, tm),
        grid=(ns, L // tm),
        in_specs=[tok(D_MODEL),
                  pl.BlockSpec((1, D_MODEL), lambda s, j: (0, 0)),
                  pl.BlockSpec((D_MODEL, IN_W), lambda s, j: (0, 0), pipeline_mode=pl.Buffered(1)),
                  pl.BlockSpec((2, depth, REC_W), lambda s, j: (0, 0, 0))],
        out_specs=[tok(ATTN_W), tok(2 * KV_W), tok(2 * KV_W), tok(REC_W), tok(REC_W), tok(REC_W),
                   tok(REC_W), tok(REC_W), tok(REC_W), tok(REC_W)],
        out_shape=[out(ATTN_W, BF16), out(2 * KV_W, BF16), out(2 * KV_W, BF16), out(REC_W, BF16),
                   out(REC_W, F32), out(REC_W, F32), out(REC_W, F32), out(REC_W, F32),
                   out(REC_W, BF16), out(REC_W, BF16)],
        scratch_shapes=[pltpu.VMEM((tm, D_MODEL), BF16)],
        compiler_params=pltpu.CompilerParams(
            dimension_semantics=("arbitrary", "arbitrary"), vmem_limit_bytes=VMEM_LIMIT),
        name="inproj",
    )(h, g, w, lb_logits)


def _t5_bucket(rel):
    half = NUM_BUCKETS // 2
    max_exact = half // 2
    ret = (rel > 0).astype(jnp.int32) * half
    n = jnp.abs(rel)
    nf = jnp.maximum(n, 1).astype(jnp.float32)
    large = max_exact + (jnp.log(nf / max_exact) / math.log(MAX_DISTANCE / max_exact)
                         * (half - max_exact)).astype(jnp.int32)
    large = jnp.clip(large, 0, half - 1)
    return ret + jnp.where(n < max_exact, n, large)


def _attn_bias(rel_bias, n_blocks):
    lp = n_blocks * BLOCK
    table = rel_bias.astype(F32)
    out = []
    for b in (0, 1, 2, n_blocks - 1):
        q_pos = b * BLOCK + jnp.arange(BLOCK)
        meta_pos = jnp.arange(BLOCK)
        band_pos = (b - 1) * BLOCK + jnp.arange(3 * BLOCK)
        k_pos = jnp.concatenate([meta_pos, band_pos])
        rel = k_pos[None, :] - q_pos[:, None]
        band_ok = (band_pos >= BLOCK) & (band_pos < lp)
        valid = jnp.concatenate([
            jnp.broadcast_to((meta_pos >= PAD)[None, :], (BLOCK, BLOCK)),
            band_ok[None, :] & (jnp.abs(rel[:, BLOCK:]) <= WINDOW)], axis=1)
        bias = jnp.transpose(table[_t5_bucket(rel)], (2, 0, 1))
        out.append(jnp.where(valid[None], bias, MASK_VALUE))
    return jnp.stack(out)


def _attn_kernel(q_ref, k0_ref, kp_ref, kc_ref, kn_ref, v0_ref, vp_ref, vc_ref, vn_ref,
                 bias_ref, sink_ref, gain_ref, o_ref):
    scale = HEAD_DIM ** -0.5
    group = N_HEADS // N_KV
    lo = lax.broadcasted_iota(jnp.int32, (BLOCK, 2 * HEAD_DIM), 1) < HEAD_DIM
    q_all = q_ref[0]
    zero = jnp.zeros((BLOCK, 2 * HEAD_DIM), BF16)
    outs = []
    for g in range(N_KV):
        ls = slice(g * 2 * HEAD_DIM, (g + 1) * 2 * HEAD_DIM)
        keys = jnp.concatenate([r[0][:, ls] for r in (k0_ref, kp_ref, kc_ref, kn_ref)], axis=0)
        vals = jnp.concatenate([r[0][:, ls] for r in (v0_ref, vp_ref, vc_ref, vn_ref)], axis=0)
        qs = []
        for p in range(group // 2):
            q128 = q_all[:, (g * group + 2 * p) * HEAD_DIM:(g * group + 2 * p + 2) * HEAD_DIM]
            qs.append(jnp.where(lo, q128, zero))
            qs.append(jnp.where(lo, zero, q128))
        qst = jnp.concatenate(qs, axis=0)
        s = lax.dot_general(qst, keys, (((1,), (1,)), ((), ())), preferred_element_type=F32)
        es, invs = [], []
        for hh in range(group):
            head = g * group + hh
            b = bias_ref[0, head]
            lg = s[hh * BLOCK:(hh + 1) * BLOCK] * scale + b
            lg = jnp.where(b > 0.5 * MASK_VALUE, lg, MASK_VALUE)
            sk = sink_ref[0, head]
            m = jnp.maximum(jnp.max(lg, axis=-1, keepdims=True), sk)
            e = jnp.exp(lg - m)
            den = jnp.sum(e, axis=-1, keepdims=True) + jnp.exp(sk - m)
            es.append(e.astype(BF16))
            invs.append(1.0 / den)
        o = jnp.dot(jnp.concatenate(es, axis=0), vals, preferred_element_type=F32)
        for p in range(group // 2):
            oa = o[(2 * p) * BLOCK:(2 * p + 1) * BLOCK] * invs[2 * p]
            ob = o[(2 * p + 1) * BLOCK:(2 * p + 2) * BLOCK] * invs[2 * p + 1]
            outs.append(jnp.where(lo, oa, ob))
    out = jnp.concatenate(outs, axis=1)
    o_ref[0] = _rms(out, gain_ref[...]).astype(BF16)


def _attention(q, k2, v2, bias, sink, gain):
    ns, L, _ = q.shape
    nb = L // BLOCK

    def kv_spec(fn):
        return pl.BlockSpec((1, BLOCK, 2 * KV_W), fn)

    kv_specs = [kv_spec(lambda s, b: (s, 0, 0)),
                kv_spec(lambda s, b: (s, jnp.maximum(b - 1, 0), 0)),
                kv_spec(lambda s, b: (s, b, 0)),
                kv_spec(lambda s, b: (s, jnp.minimum(b + 1, nb - 1), 0))]

    def cls(s, b):
        c = jnp.where(b == 0, 0, jnp.where(b == 1, 1, jnp.where(b == nb - 1, 3, 2)))
        return (c, 0, 0, 0)

    return pl.pallas_call(
        _attn_kernel,
        grid=(ns, nb),
        in_specs=[pl.BlockSpec((1, BLOCK, ATTN_W), lambda s, b: (s, b, 0))] + kv_specs + kv_specs + [
            pl.BlockSpec((1, N_HEADS, BLOCK, 4 * BLOCK), cls),
            pl.BlockSpec(memory_space=pltpu.SMEM),
            pl.BlockSpec((1, ATTN_W), lambda s, b: (0, 0))],
        out_specs=pl.BlockSpec((1, BLOCK, ATTN_W), lambda s, b: (s, b, 0)),
        out_shape=jax.ShapeDtypeStruct((ns, L, ATTN_W), BF16),
        compiler_params=pltpu.CompilerParams(
            dimension_semantics=("arbitrary", "arbitrary"), vmem_limit_bytes=VMEM_LIMIT),
        name="attn",
    )(q, k2, k2, k2, k2, v2, v2, v2, v2, bias, sink, gain)


def _level_masks():
    r = np.arange(CHUNK)[:, None]
    c = np.arange(CHUNK)[None, :]
    fwd = []
    for l in range(N_LEVELS):
        same = (r >> (l + 1)) == (c >> (l + 1))
        fwd.append(same & (((r >> l) & 1) == 1) & (((c >> l) & 1) == 0))
    fwd.append(r == c)
    fwd = np.stack(fwd).astype(np.float32)
    return np.stack([fwd, np.transpose(fwd, (0, 2, 1))])


def _block_bcast(w, l, reverse, off):
    h = 1 << l
    if h >= 8:
        pieces = []
        for base in range(0, CHUNK, 2 * h):
            src = base + (h if reverse else h - 1)
            pieces.append(jnp.broadcast_to(w[src:src + 1, :], (2 * h, w.shape[1])))
        return jnp.concatenate(pieces, axis=0)
    o = off & (2 * h - 1)
    a = w
    s = 1
    while s < h:
        if reverse:
            a = jnp.where((o >= h + s) & (o < h + 2 * s), pltpu.roll(a, s, 0), a)
        else:
            a = jnp.where((o >= h - 2 * s) & (o < h - s), pltpu.roll(a, CHUNK - s, 0), a)
        s *= 2
    if reverse:
        return jnp.where(o < h, pltpu.roll(a, CHUNK - h, 0), a)
    return jnp.where(o >= h, pltpu.roll(a, h, 0), a)


def _hgrn_chunk(q, k, g, v, vt, state_t, masks_ref, reverse):
    off = lax.broadcasted_iota(jnp.int32, (CHUNK, REC_DK), 0)
    nt = (((1,), (1,)), ((), ()))
    w = g
    scores = masks_ref[N_LEVELS] * lax.dot_general(q.astype(BF16), k.astype(BF16), nt, preferred_element_type=F32)
    for l in range(N_LEVELS):
        later = ((off >> l) & 1) == 1
        piv = _block_bcast(w, l, reverse, off)
        if reverse:
            expo = jnp.where(later, piv - w, w)
            w_next = w + jnp.where(later, 0.0, piv)
        else:
            expo = jnp.where(later, w, piv - w)
            w_next = w + jnp.where(later, piv, 0.0)
        e = jnp.exp(expo)
        sl = lax.dot_general((q * e).astype(BF16), (k * e).astype(BF16), nt, preferred_element_type=F32)
        scores = scores + masks_ref[l] * sl
        w = w_next
    edge = w[0:1, :] if reverse else w[CHUNK - 1:CHUNK, :]
    intra = jnp.dot(scores.astype(BF16), v, preferred_element_type=F32)
    inter = lax.dot_general((q * jnp.exp(w)).astype(BF16), state_t.astype(BF16), nt, preferred_element_type=F32)
    khat = (k * jnp.exp(edge - w)).astype(BF16)
    new_state = state_t * jnp.exp(edge) + jnp.dot(vt, khat, preferred_element_type=F32)
    return inter + intra, new_state


def _hgrn_kernel(n_chunks, q_ref, lff_ref, kf_ref, lfb_ref, kb_ref, v_ref, sg_ref, gain_ref, masks_ref,
                 o_ref, ob_ref):
    def load(c):
        r0 = pl.multiple_of(c * CHUNK, CHUNK)
        rows = pl.ds(r0, CHUNK)
        q = q_ref[0, rows, :].astype(F32)
        v = v_ref[0, rows, :]
        vt = v.astype(F32).T.astype(BF16)
        return rows, q, v, vt

    def bwd(i, state_t):
        rows, q, v, vt = load(n_chunks - 1 - i)
        o, state_t = _hgrn_chunk(q, kb_ref[0, rows, :], lfb_ref[0, rows, :], v, vt, state_t,
                                 masks_ref.at[1], True)
        ob_ref[rows, :] = o
        return state_t

    lax.fori_loop(0, n_chunks, bwd, jnp.zeros((REC_DV, REC_DK), F32))

    def fwd(c, state_t):
        rows, q, v, vt = load(c)
        o, state_t = _hgrn_chunk(q, kf_ref[0, rows, :], lff_ref[0, rows, :], v, vt, state_t,
                                 masks_ref.at[0], False)
        o = o + ob_ref[rows, :]
        o = _rms(o, gain_ref[...]) * sg_ref[0, rows, :].astype(F32)
        o_ref[0, rows, :] = o.astype(BF16)
        return state_t

    lax.fori_loop(0, n_chunks, fwd, jnp.zeros((REC_DV, REC_DK), F32))


def _hgrn(qr, lff, kf, lfb, kb, ir, sg, gain, masks):
    ns, L, _ = qr.shape
    head = pl.BlockSpec((1, L, REC_DK), lambda s, hd: (s, 0, hd))
    return pl.pallas_call(
        functools.partial(_hgrn_kernel, L // CHUNK),
        grid=(ns, REC_HEADS),
        in_specs=[head, head, head, head, head, head, head,
                  pl.BlockSpec((1, REC_DV), lambda s, hd: (0, hd)),
                  pl.BlockSpec((2, N_LEVELS + 1, CHUNK, CHUNK), lambda s, hd: (0, 0, 0, 0))],
        out_specs=head,
        out_shape=jax.ShapeDtypeStruct((ns, L, REC_W), BF16),
        scratch_shapes=[pltpu.VMEM((L, REC_DV), F32)],
        compiler_params=pltpu.CompilerParams(
            dimension_semantics=("arbitrary", "arbitrary"), vmem_limit_bytes=VMEM_LIMIT),
        name="hgrn",
    )(qr, lff, kf, lfb, kb, ir, sg, gain, masks)


def _outproj_kernel(x_ref, a_ref, r_ref, w_ref, g_ref, o_ref):
    mix = (jnp.dot(a_ref[0], w_ref[:ATTN_W, :], preferred_element_type=F32)
           + jnp.dot(r_ref[0], w_ref[ATTN_W:, :], preferred_element_type=F32))
    o_ref[0] = x_ref[0] + _rms(mix, g_ref[...])


def _outproj(h, attn, rec, w, g):
    ns, L, _ = h.shape
    tm = _pick_tm(L)

    def tok(width):
        return pl.BlockSpec((1, tm, width), lambda s, j: (s, j, 0))

    return pl.pallas_call(
        _outproj_kernel,
        grid=(ns, L // tm),
        in_specs=[tok(D_MODEL), tok(ATTN_W), tok(REC_W),
                  pl.BlockSpec((ATTN_W + REC_W, D_MODEL), lambda s, j: (0, 0), pipeline_mode=pl.Buffered(1)),
                  pl.BlockSpec((1, D_MODEL), lambda s, j: (0, 0))],
        out_specs=tok(D_MODEL),
        out_shape=jax.ShapeDtypeStruct(h.shape, F32),
        input_output_aliases={0: 0},
        compiler_params=pltpu.CompilerParams(
            dimension_semantics=("arbitrary", "arbitrary"), vmem_limit_bytes=VMEM_LIMIT),
        name="outproj",
    )(h, attn, rec, w, g)


def _prep_ffn(w_in, w_out):
    a = w_in[:, :D_FF].reshape(D_MODEL, N_FF_CHUNKS, FF_CHUNK)
    b = w_in[:, D_FF:].reshape(D_MODEL, N_FF_CHUNKS, FF_CHUNK)
    win_c = jnp.transpose(jnp.concatenate([a, b], axis=2), (1, 0, 2)).astype(BF16)
    wout_c = w_out.reshape(N_FF_CHUNKS, FF_CHUNK, D_MODEL).astype(BF16)
    return win_c, wout_c


def _prep_in(w):
    q = w[:, :ATTN_W]
    k = w[:, ATTN_W:ATTN_W + KV_W]
    v = w[:, ATTN_W + KV_W:ATTN_W + 2 * KV_W]
    rest = w[:, ATTN_W + 2 * KV_W:]

    def dup(t):
        return jnp.concatenate([t[:, :HEAD_DIM], t[:, :HEAD_DIM], t[:, HEAD_DIM:], t[:, HEAD_DIM:]], axis=1)

    return jnp.concatenate([q, dup(k), dup(v), rest], axis=1).astype(BF16)


def kernel(x_prompt, x_sample, meta_tokens, rel_bias, lb_logits, norm_gains, w_in, attn_sink, attn_out_gain,
           rec_out_gain, w_out, ffn1_w_in, ffn1_w_out, ffn2_w_in, ffn2_w_out):
    depth = w_in.shape[0]
    n_prompt = x_prompt.shape[0]
    x = jnp.concatenate([x_prompt, x_sample], axis=0)
    ns, seq, _ = x.shape
    h = jnp.concatenate([jnp.zeros((ns, PAD, D_MODEL), x.dtype),
                         jnp.broadcast_to(meta_tokens.astype(x.dtype)[None], (ns, N_META, D_MODEL)),
                         x], axis=1)
    n_blocks = h.shape[1] // BLOCK
    bias = _attn_bias(rel_bias, n_blocks)
    masks = jnp.asarray(_level_masks())
    lbl = lb_logits.astype(F32)
    for l in range(depth):
        g = norm_gains[l].astype(F32)
        h = _ffn(h, g[0:1], g[1:2], *_prep_ffn(ffn1_w_in[l], ffn1_w_out[l]))
        qa, k2, v2, qr, lff, kf, lfb, kb, ir, sg = _inproj(h, g[2:3], _prep_in(w_in[l]), lbl, l)
        attn = _attention(qa, k2, v2, bias, attn_sink[l:l + 1].astype(F32), attn_out_gain[l:l + 1].astype(F32))
        rec = _hgrn(qr, lff, kf, lfb, kb, ir, sg, rec_out_gain[l:l + 1].astype(F32), masks)
        h = _outproj(h, attn, rec, w_out[l].astype(BF16), g[3:4])
        h = _ffn(h, g[4:5], g[5:6], *_prep_ffn(ffn2_w_in[l], ffn2_w_out[l]))
    y = h[:, PAD + N_META:]
    return (y[:n_prompt], y[n_prompt:])
```

```python
import functools
import math

import numpy as np
import jax
import jax.numpy as jnp
from jax import lax
from jax.experimental import pallas as pl
from jax.experimental.pallas import tpu as pltpu

F32 = jnp.float32
BF16 = jnp.bfloat16

D_MODEL = 1024
N_META = 16
BLOCK = 128
PAD = BLOCK - N_META
WINDOW = 128
HEAD_DIM = 64
N_HEADS = 8
N_KV = 2
ATTN_W = N_HEADS * HEAD_DIM
KV_W = N_KV * HEAD_DIM
NUM_BUCKETS = 32
MAX_DISTANCE = 128
MASK_VALUE = -1e30
REC_HEADS = 4
REC_DK = 128
REC_DV = 128
REC_W = REC_HEADS * REC_DK
MIN_FORGET = 1e-30
D_FF = 2816
FFN_RESIDUAL = 0.5
EPS = 1e-6

FF_CHUNK = 256
N_FF_CHUNKS = D_FF // FF_CHUNK
CHUNK = 64
N_LEVELS = 6
SUB = 16
SUB_LEVEL = 4
SUB_DECAY_LIMIT = 60.0
IN_W = ATTN_W + 4 * KV_W + 5 * REC_W
VMEM_LIMIT = 52 * 1024 * 1024


def _rms(x, g):
    ms = jnp.mean(x * x, axis=-1, keepdims=True)
    return x * lax.rsqrt(ms + EPS) * g


def _pick_tm(seq_len):
    for tm in range(704, 15, -16):
        if seq_len % tm == 0:
            return tm
    raise ValueError(f"no token tile for sequence length {seq_len}")


def _ffn_kernel(x_ref, gpre_ref, gpost_ref, win_ref, wout_ref, o_ref, xn_ref, acc_ref):
    xn_ref[...] = _rms(x_ref[0], gpre_ref[...]).astype(BF16)
    for c in range(N_FF_CHUNKS):
        ab = jnp.dot(xn_ref[...], win_ref[c], preferred_element_type=F32)
        a = ab[:, :FF_CHUNK]
        b = ab[:, FF_CHUNK:]
        s = (a * jax.nn.sigmoid(a) * b).astype(BF16)
        y = jnp.dot(s, wout_ref[c], preferred_element_type=F32)
        if c == 0:
            acc_ref[...] = y
        else:
            acc_ref[...] += y
    o_ref[0] = x_ref[0] + FFN_RESIDUAL * _rms(acc_ref[...], gpost_ref[...])


def _ffn(h, gpre, gpost, win_c, wout_c):
    ns, L, _ = h.shape
    tm = _pick_tm(L)
    tok = pl.BlockSpec((1, tm, D_MODEL), lambda s, j: (s, j, 0))
    gain = pl.BlockSpec((1, D_MODEL), lambda s, j: (0, 0))
    return pl.pallas_call(
        _ffn_kernel,
        grid=(ns, L // tm),
        in_specs=[tok, gain, gain,
                  pl.BlockSpec((N_FF_CHUNKS, D_MODEL, 2 * FF_CHUNK), lambda s, j: (0, 0, 0),
                               pipeline_mode=pl.Buffered(1)),
                  pl.BlockSpec((N_FF_CHUNKS, FF_CHUNK, D_MODEL), lambda s, j: (0, 0, 0),
                               pipeline_mode=pl.Buffered(1))],
        out_specs=tok,
        out_shape=jax.ShapeDtypeStruct(h.shape, F32),
        scratch_shapes=[pltpu.VMEM((tm, D_MODEL), BF16), pltpu.VMEM((tm, D_MODEL), F32)],
        input_output_aliases={0: 0},
        compiler_params=pltpu.CompilerParams(
            dimension_semantics=("arbitrary", "arbitrary"), vmem_limit_bytes=VMEM_LIMIT),
        name="ffn",
    )(h, gpre, gpost, win_c, wout_c)


def _lower_bound(lbl, layer):
    m = jnp.max(lbl, axis=0, keepdims=True)
    e = jnp.exp(lbl - m)
    den = jnp.sum(e, axis=0, keepdims=True)
    if layer == 0:
        return jnp.zeros_like(den)
    return jnp.sum(e[1:layer + 1], axis=0, keepdims=True) / den


def _inproj_kernel(layer, tm, x_ref, g_ref, w_ref, lbl_ref,
                   q_ref, k_ref, v_ref, qr_ref, lff_ref, kf_ref, lfb_ref, kb_ref, ir_ref, sg_ref, u_ref):
    u_ref[...] = _rms(x_ref[0], g_ref[...]).astype(BF16)

    def proj(c0, n):
        return jnp.dot(u_ref[...], w_ref[:, c0:c0 + n], preferred_element_type=F32)

    q_ref[0] = proj(0, ATTN_W).astype(BF16)
    kv = proj(ATTN_W, 4 * KV_W)
    k_ref[0] = kv[:, :2 * KV_W].astype(BF16)
    v_ref[0] = kv[:, 2 * KV_W:].astype(BF16)
    c0 = ATTN_W + 4 * KV_W
    qr_ref[0] = proj(c0, REC_W).astype(BF16)

    rows = pl.program_id(1) * tm + lax.broadcasted_iota(jnp.int32, (tm, 1), 0)
    tok = rows >= PAD
    for d, (lf_ref, kk_ref) in enumerate(((lff_ref, kf_ref), (lfb_ref, kb_ref))):
        z = proj(c0 + (1 + d) * REC_W, REC_W)
        lb = _lower_bound(lbl_ref[d], layer)
        e = jnp.exp(-jnp.abs(z))
        r = 1.0 / (1.0 + e)
        er = e * r
        pos = z >= 0.0
        sig_p = jnp.where(pos, r, er)
        sig_n = jnp.where(pos, er, r)
        f = lb + (1.0 - lb) * sig_p
        lf_ref[0] = jnp.where(tok, jnp.log(jnp.maximum(f, MIN_FORGET)), 0.0)
        kk_ref[0] = jnp.where(tok, (1.0 - lb) * sig_n, 0.0)
    ir_ref[0] = proj(c0 + 3 * REC_W, REC_W).astype(BF16)
    gz = proj(c0 + 4 * REC_W, REC_W)
    sg_ref[0] = (gz * jax.nn.sigmoid(gz)).astype(BF16)


def _inproj(h, g, w, lb_logits, layer):
    ns, L, _ = h.shape
    tm = _pick_tm(L)
    depth = lb_logits.shape[1]

    def tok(width):
        return pl.BlockSpec((1, tm, width), lambda s, j: (s, j, 0))

    def out(width, dt):
        return jax.ShapeDtypeStruct((ns, L, width), dt)

    return pl.pallas_call(
        functools.partial(_inproj_kernel, layer, tm),
        grid=(ns, L // tm),
        in_specs=[tok(D_MODEL),
                  pl.BlockSpec((1, D_MODEL), lambda s, j: (0, 0)),
                  pl.BlockSpec((D_MODEL, IN_W), lambda s, j: (0, 0), pipeline_mode=pl.Buffered(1)),
                  pl.BlockSpec((2, depth, REC_W), lambda s, j: (0, 0, 0))],
        out_specs=[tok(ATTN_W), tok(2 * KV_W), tok(2 * KV_W), tok(REC_W), tok(REC_W), tok(REC_W),
                   tok(REC_W), tok(REC_W), tok(REC_W), tok(REC_W)],
        out_shape=[out(ATTN_W, BF16), out(2 * KV_W, BF16), out(2 * KV_W, BF16), out(REC_W, BF16),
                   out(REC_W, F32), out(REC_W, F32), out(REC_W, F32), out(REC_W, F32),
                   out(REC_W, BF16), out(REC_W, BF16)],
        scratch_shapes=[pltpu.VMEM((tm, D_MODEL), BF16)],
        compiler_params=pltpu.CompilerParams(
            dimension_semantics=("arbitrary", "arbitrary"), vmem_limit_bytes=VMEM_LIMIT),
        name="inproj",
    )(h, g, w, lb_logits)


def _t5_bucket(rel):
    half = NUM_BUCKETS // 2
    max_exact = half // 2
    ret = (rel > 0).astype(jnp.int32) * half
    n = jnp.abs(rel)
    nf = jnp.maximum(n, 1).astype(jnp.float32)
    large = max_exact + (jnp.log(nf / max_exact) / math.log(MAX_DISTANCE / max_exact)
                         * (half - max_exact)).astype(jnp.int32)
    large = jnp.clip(large, 0, half - 1)
    return ret + jnp.where(n < max_exact, n, large)


def _attn_bias(rel_bias, n_blocks):
    lp = n_blocks * BLOCK
    table = rel_bias.astype(F32)
    out = []
    for b in (0, 1, 2, n_blocks - 1):
        q_pos = b * BLOCK + jnp.arange(BLOCK)
        meta_pos = jnp.arange(BLOCK)
        band_pos = (b - 1) * BLOCK + jnp.arange(3 * BLOCK)
        k_pos = jnp.concatenate([meta_pos, band_pos])
        rel = k_pos[None, :] - q_pos[:, None]
        band_ok = (band_pos >= BLOCK) & (band_pos < lp)
        valid = jnp.concatenate([
            jnp.broadcast_to((meta_pos >= PAD)[None, :], (BLOCK, BLOCK)),
            band_ok[None, :] & (jnp.abs(rel[:, BLOCK:]) <= WINDOW)], axis=1)
        bias = jnp.transpose(table[_t5_bucket(rel)], (2, 0, 1))
        out.append(jnp.where(valid[None], bias, MASK_VALUE))
    return jnp.stack(out)


def _attn_kernel(q_ref, k0_ref, kp_ref, kc_ref, kn_ref, v0_ref, vp_ref, vc_ref, vn_ref,
                 bias_ref, sink_ref, gain_ref, o_ref):
    scale = HEAD_DIM ** -0.5
    group = N_HEADS // N_KV
    lo = lax.broadcasted_iota(jnp.int32, (BLOCK, 2 * HEAD_DIM), 1) < HEAD_DIM
    q_all = q_ref[0]
    zero = jnp.zeros((BLOCK, 2 * HEAD_DIM), BF16)
    outs = []
    for g in range(N_KV):
        ls = slice(g * 2 * HEAD_DIM, (g + 1) * 2 * HEAD_DIM)
        keys = jnp.concatenate([r[0][:, ls] for r in (k0_ref, kp_ref, kc_ref, kn_ref)], axis=0)
        vals = jnp.concatenate([r[0][:, ls] for r in (v0_ref, vp_ref, vc_ref, vn_ref)], axis=0)
        qs = []
        for p in range(group // 2):
            q128 = q_all[:, (g * group + 2 * p) * HEAD_DIM:(g * group + 2 * p + 2) * HEAD_DIM]
            qs.append(jnp.where(lo, q128, zero))
            qs.append(jnp.where(lo, zero, q128))
        qst = jnp.concatenate(qs, axis=0)
        s = lax.dot_general(qst, keys, (((1,), (1,)), ((), ())), preferred_element_type=F32)
        es, invs = [], []
        for hh in range(group):
            head = g * group + hh
            b = bias_ref[0, head]
            lg = s[hh * BLOCK:(hh + 1) * BLOCK] * scale + b
            lg = jnp.where(b > 0.5 * MASK_VALUE, lg, MASK_VALUE)
            sk = sink_ref[0, head]
            m = jnp.maximum(jnp.max(lg, axis=-1, keepdims=True), sk)
            e = jnp.exp(lg - m)
            den = jnp.sum(e, axis=-1, keepdims=True) + jnp.exp(sk - m)
            es.append(e.astype(BF16))
            invs.append(1.0 / den)
        o = jnp.dot(jnp.concatenate(es, axis=0), vals, preferred_element_type=F32)
        for p in range(group // 2):
            oa = o[(2 * p) * BLOCK:(2 * p + 1) * BLOCK] * invs[2 * p]
            ob = o[(2 * p + 1) * BLOCK:(2 * p + 2) * BLOCK] * invs[2 * p + 1]
            outs.append(jnp.where(lo, oa, ob))
    out = jnp.concatenate(outs, axis=1)
    o_ref[0] = _rms(out, gain_ref[...]).astype(BF16)


def _attention(q, k2, v2, bias, sink, gain):
    ns, L, _ = q.shape
    nb = L // BLOCK

    def kv_spec(fn):
        return pl.BlockSpec((1, BLOCK, 2 * KV_W), fn)

    kv_specs = [kv_spec(lambda s, b: (s, 0, 0)),
                kv_spec(lambda s, b: (s, jnp.maximum(b - 1, 0), 0)),
                kv_spec(lambda s, b: (s, b, 0)),
                kv_spec(lambda s, b: (s, jnp.minimum(b + 1, nb - 1), 0))]

    def cls(s, b):
        c = jnp.where(b == 0, 0, jnp.where(b == 1, 1, jnp.where(b == nb - 1, 3, 2)))
        return (c, 0, 0, 0)

    return pl.pallas_call(
        _attn_kernel,
        grid=(ns, nb),
        in_specs=[pl.BlockSpec((1, BLOCK, ATTN_W), lambda s, b: (s, b, 0))] + kv_specs + kv_specs + [
            pl.BlockSpec((1, N_HEADS, BLOCK, 4 * BLOCK), cls),
            pl.BlockSpec(memory_space=pltpu.SMEM),
            pl.BlockSpec((1, ATTN_W), lambda s, b: (0, 0))],
        out_specs=pl.BlockSpec((1, BLOCK, ATTN_W), lambda s, b: (s, b, 0)),
        out_shape=jax.ShapeDtypeStruct((ns, L, ATTN_W), BF16),
        compiler_params=pltpu.CompilerParams(
            dimension_semantics=("arbitrary", "arbitrary"), vmem_limit_bytes=VMEM_LIMIT),
        name="attn",
    )(q, k2, k2, k2, k2, v2, v2, v2, v2, bias, sink, gain)


def _hgrn_consts():
    r = np.arange(CHUNK)[:, None]
    c = np.arange(CHUNK)[None, :]
    fwd = []
    for l in range(N_LEVELS):
        same = (r >> (l + 1)) == (c >> (l + 1))
        fwd.append(same & (((r >> l) & 1) == 1) & (((c >> l) & 1) == 0))
    fwd.append(r == c)
    fwd.append(((r // SUB) == (c // SUB)) & (c <= r))
    fwd = np.stack(fwd).astype(np.float32)
    masks = np.stack([fwd, np.transpose(fwd, (0, 2, 1))])
    tri = np.stack([(c <= r), (c >= r)]).astype(np.float32)
    return masks, tri


def _block_bcast(w, l, reverse, off):
    h = 1 << l
    if h >= 8:
        pieces = []
        for base in range(0, CHUNK, 2 * h):
            src = base + (h if reverse else h - 1)
            pieces.append(jnp.broadcast_to(w[src:src + 1, :], (2 * h, w.shape[1])))
        return jnp.concatenate(pieces, axis=0)
    o = off & (2 * h - 1)
    a = w
    s = 1
    while s < h:
        if reverse:
            a = jnp.where((o >= h + s) & (o < h + 2 * s), pltpu.roll(a, s, 0), a)
        else:
            a = jnp.where((o >= h - 2 * s) & (o < h - s), pltpu.roll(a, CHUNK - s, 0), a)
        s *= 2
    if reverse:
        return jnp.where(o < h, pltpu.roll(a, CHUNK - h, 0), a)
    return jnp.where(o >= h, pltpu.roll(a, h, 0), a)


_NT = (((1,), (1,)), ((), ()))


def _pair_scores(qe, ke):
    return lax.dot_general(qe.astype(BF16), ke.astype(BF16), _NT, preferred_element_type=F32)


def _hgrn_finish(q, k, v, vt, w, scores, state_t, reverse):
    edge = w[0:1, :] if reverse else w[CHUNK - 1:CHUNK, :]
    intra = jnp.dot(scores.astype(BF16), v, preferred_element_type=F32)
    inter = lax.dot_general((q * jnp.exp(w)).astype(BF16), state_t.astype(BF16), _NT, preferred_element_type=F32)
    khat = (k * jnp.exp(edge - w)).astype(BF16)
    new_state = state_t * jnp.exp(edge) + jnp.dot(vt, khat, preferred_element_type=F32)
    return inter + intra, new_state


def _hgrn_chunk_any(q, k, g, v, vt, state_t, masks_ref, reverse):
    off = lax.broadcasted_iota(jnp.int32, (CHUNK, REC_DK), 0)
    w = g
    scores = masks_ref[N_LEVELS] * _pair_scores(q, k)
    for l in range(N_LEVELS):
        later = ((off >> l) & 1) == 1
        piv = _block_bcast(w, l, reverse, off)
        if reverse:
            expo = jnp.where(later, piv - w, w)
            w_next = w + jnp.where(later, 0.0, piv)
        else:
            expo = jnp.where(later, w, piv - w)
            w_next = w + jnp.where(later, piv, 0.0)
        e = jnp.exp(expo)
        scores = scores + masks_ref[l] * _pair_scores(q * e, k * e)
        w = w_next
    return _hgrn_finish(q, k, v, vt, w, scores, state_t, reverse)


def _hgrn_chunk_bounded(q, k, w, v, vt, state_t, masks_ref, reverse):
    scores = None
    for l in range(N_LEVELS - 1, SUB_LEVEL - 1, -1):
        e = jnp.exp(-jnp.abs(w - _block_bcast(w, l, reverse, None)))
        sl = jnp.where(masks_ref[l] > 0.5, _pair_scores(q * e, k * e), 0.0)
        scores = sl if scores is None else scores + sl
    zeros = jnp.zeros((SUB, w.shape[1]), F32)
    pieces = []
    for base in range(0, CHUNK, SUB):
        src = base + SUB if reverse else base - 1
        if 0 <= src < CHUNK:
            pieces.append(jnp.broadcast_to(w[src:src + 1, :], (SUB, w.shape[1])))
        else:
            pieces.append(zeros)
    d = w - jnp.concatenate(pieces, axis=0)
    sl = _pair_scores(q * jnp.exp(d), k * jnp.exp(-d))
    scores = scores + jnp.where(masks_ref[N_LEVELS + 1] > 0.5, sl, 0.0)
    return _hgrn_finish(q, k, v, vt, w, scores, state_t, reverse)


def _split_dot(tri, g):
    hi = g.astype(BF16)
    lo = (g - hi.astype(F32)).astype(BF16)
    return jnp.dot(tri, hi, preferred_element_type=F32) + jnp.dot(tri, lo, preferred_element_type=F32)


def _hgrn_kernel(n_chunks, q_ref, lff_ref, kf_ref, lfb_ref, kb_ref, v_ref, sg_ref, gain_ref, masks_ref, tri_ref,
                 o_ref, wf_ref, wb_ref, of_ref, ob_ref):
    def chunk_rows(c):
        return pl.ds(pl.multiple_of(c * CHUNK, CHUNK), CHUNK)

    def prepass(c, worst):
        rows = chunk_rows(c)
        wf = _split_dot(tri_ref[0], lff_ref[0, rows, :])
        wb = _split_dot(tri_ref[1], lfb_ref[0, rows, :])
        wf_ref[rows, :] = wf
        wb_ref[rows, :] = wb
        for i in range(CHUNK // SUB):
            last = (i + 1) * SUB - 1
            tf = wf[last:last + 1, :] - (wf[last - SUB:last - SUB + 1, :] if i > 0 else 0.0)
            first = i * SUB
            tb = wb[first:first + 1, :] - (wb[first + SUB:first + SUB + 1, :] if first + SUB < CHUNK else 0.0)
            worst = jnp.minimum(worst, jnp.minimum(tf, tb))
        return worst

    worst = lax.fori_loop(0, n_chunks, prepass, jnp.zeros((1, REC_DK), F32), unroll=math.gcd(n_chunks, 6))
    bounded = jnp.min(worst) >= -SUB_DECAY_LIMIT

    def load(c):
        rows = chunk_rows(c)
        q = q_ref[0, rows, :].astype(F32)
        v = v_ref[0, rows, :]
        vt = v.astype(F32).T.astype(BF16)
        return rows, q, v, vt

    def sweep(step, unroll):
        def body(i, carry):
            sf, sb = carry
            rows, q, v, vt = load(i)
            o, sf = step(rows, q, v, vt, sf, False)
            of_ref[rows, :] = o
            rows, q, v, vt = load(n_chunks - 1 - i)
            o, sb = step(rows, q, v, vt, sb, True)
            ob_ref[rows, :] = o
            return sf, sb

        zero = jnp.zeros((REC_DV, REC_DK), F32)
        lax.fori_loop(0, n_chunks, body, (zero, zero), unroll=math.gcd(n_chunks, unroll))

    @pl.when(bounded)
    def _():
        def step(rows, q, v, vt, st, reverse):
            k_ref, w_ref = (kb_ref, wb_ref) if reverse else (kf_ref, wf_ref)
            return _hgrn_chunk_bounded(q, k_ref[0, rows, :], w_ref[rows, :], v, vt, st,
                                       masks_ref.at[1 if reverse else 0], reverse)
        sweep(step, 6)

    @pl.when(jnp.logical_not(bounded))
    def _():
        def step(rows, q, v, vt, st, reverse):
            k_ref, g_ref = (kb_ref, lfb_ref) if reverse else (kf_ref, lff_ref)
            return _hgrn_chunk_any(q, k_ref[0, rows, :], g_ref[0, rows, :], v, vt, st,
                                   masks_ref.at[1 if reverse else 0], reverse)
        sweep(step, 2)

    def epilogue(t, carry):
        rows = pl.ds(pl.multiple_of(t * BLOCK, BLOCK), BLOCK)
        o = of_ref[rows, :] + ob_ref[rows, :]
        o = _rms(o, gain_ref[...]) * sg_ref[0, rows, :].astype(F32)
        o_ref[0, rows, :] = o.astype(BF16)
        return carry

    n_tiles = n_chunks * CHUNK // BLOCK
    lax.fori_loop(0, n_tiles, epilogue, 0, unroll=math.gcd(n_tiles, 3))


def _hgrn(qr, lff, kf, lfb, kb, ir, sg, gain, masks, tri):
    ns, L, _ = qr.shape
    head = pl.BlockSpec((1, L, REC_DK), lambda s, hd: (s, 0, hd))
    scratch = pltpu.VMEM((L, REC_DV), F32)
    return pl.pallas_call(
        functools.partial(_hgrn_kernel, L // CHUNK),
        grid=(ns, REC_HEADS),
        in_specs=[head, head, head, head, head, head, head,
                  pl.BlockSpec((1, REC_DV), lambda s, hd: (0, hd)),
                  pl.BlockSpec(masks.shape, lambda s, hd: (0, 0, 0, 0)),
                  pl.BlockSpec(tri.shape, lambda s, hd: (0, 0, 0))],
        out_specs=head,
        out_shape=jax.ShapeDtypeStruct((ns, L, REC_W), BF16),
        scratch_shapes=[scratch, scratch, scratch, scratch],
        compiler_params=pltpu.CompilerParams(
            dimension_semantics=("arbitrary", "arbitrary"), vmem_limit_bytes=VMEM_LIMIT),
        name="hgrn",
    )(qr, lff, kf, lfb, kb, ir, sg, gain, masks, tri)


def _outproj_kernel(x_ref, a_ref, r_ref, w_ref, g_ref, o_ref):
    mix = (jnp.dot(a_ref[0], w_ref[:ATTN_W, :], preferred_element_type=F32)
           + jnp.dot(r_ref[0], w_ref[ATTN_W:, :], preferred_element_type=F32))
    o_ref[0] = x_ref[0] + _rms(mix, g_ref[...])


def _outproj(h, attn, rec, w, g):
    ns, L, _ = h.shape
    tm = _pick_tm(L)

    def tok(width):
        return pl.BlockSpec((1, tm, width), lambda s, j: (s, j, 0))

    return pl.pallas_call(
        _outproj_kernel,
        grid=(ns, L // tm),
        in_specs=[tok(D_MODEL), tok(ATTN_W), tok(REC_W),
                  pl.BlockSpec((ATTN_W + REC_W, D_MODEL), lambda s, j: (0, 0), pipeline_mode=pl.Buffered(1)),
                  pl.BlockSpec((1, D_MODEL), lambda s, j: (0, 0))],
        out_specs=tok(D_MODEL),
        out_shape=jax.ShapeDtypeStruct(h.shape, F32),
        input_output_aliases={0: 0},
        compiler_params=pltpu.CompilerParams(
            dimension_semantics=("arbitrary", "arbitrary"), vmem_limit_bytes=VMEM_LIMIT),
        name="outproj",
    )(h, attn, rec, w, g)


def _prep_ffn(w_in, w_out):
    a = w_in[:, :D_FF].reshape(D_MODEL, N_FF_CHUNKS, FF_CHUNK)
    b = w_in[:, D_FF:].reshape(D_MODEL, N_FF_CHUNKS, FF_CHUNK)
    win_c = jnp.transpose(jnp.concatenate([a, b], axis=2), (1, 0, 2)).astype(BF16)
    wout_c = w_out.reshape(N_FF_CHUNKS, FF_CHUNK, D_MODEL).astype(BF16)
    return win_c, wout_c


def _prep_in(w):
    q = w[:, :ATTN_W]
    k = w[:, ATTN_W:ATTN_W + KV_W]
    v = w[:, ATTN_W + KV_W:ATTN_W + 2 * KV_W]
    rest = w[:, ATTN_W + 2 * KV_W:]

    def dup(t):
        return jnp.concatenate([t[:, :HEAD_DIM], t[:, :HEAD_DIM], t[:, HEAD_DIM:], t[:, HEAD_DIM:]], axis=1)

    return jnp.concatenate([q, dup(k), dup(v), rest], axis=1).astype(BF16)


def kernel(x_prompt, x_sample, meta_tokens, rel_bias, lb_logits, norm_gains, w_in, attn_sink, attn_out_gain,
           rec_out_gain, w_out, ffn1_w_in, ffn1_w_out, ffn2_w_in, ffn2_w_out):
    depth = w_in.shape[0]
    n_prompt = x_prompt.shape[0]
    x = jnp.concatenate([x_prompt, x_sample], axis=0)
    ns, seq, _ = x.shape
    h = jnp.concatenate([jnp.zeros((ns, PAD, D_MODEL), x.dtype),
                         jnp.broadcast_to(meta_tokens.astype(x.dtype)[None], (ns, N_META, D_MODEL)),
                         x], axis=1)
    n_blocks = h.shape[1] // BLOCK
    bias = _attn_bias(rel_bias, n_blocks)
    masks_np, tri_np = _hgrn_consts()
    masks = jnp.asarray(masks_np)
    tri = jnp.asarray(tri_np, dtype=BF16)
    lbl = lb_logits.astype(F32)
    for l in range(depth):
        g = norm_gains[l].astype(F32)
        h = _ffn(h, g[0:1], g[1:2], *_prep_ffn(ffn1_w_in[l], ffn1_w_out[l]))
        qa, k2, v2, qr, lff, kf, lfb, kb, ir, sg = _inproj(h, g[2:3], _prep_in(w_in[l]), lbl, l)
        attn = _attention(qa, k2, v2, bias, attn_sink[l:l + 1].astype(F32), attn_out_gain[l:l + 1].astype(F32))
        rec = _hgrn(qr, lff, kf, lfb, kb, ir, sg, rec_out_gain[l:l + 1].astype(F32), masks, tri)
        h = _outproj(h, attn, rec, w_out[l].astype(BF16), g[3:4])
        h = _ffn(h, g[4:5], g[5:6], *_prep_ffn(ffn2_w_in[l], ffn2_w_out[l]))
    y = h[:, PAD + N_META:]
    return (y[:n_prompt], y[n_prompt:])
```

```python
import functools
import math

import numpy as np
import jax
import jax.numpy as jnp
from jax import lax
from jax.experimental import pallas as pl
from jax.experimental.pallas import tpu as pltpu

F32 = jnp.float32
BF16 = jnp.bfloat16

D_MODEL = 1024
N_META = 16
BLOCK = 128
PAD = BLOCK - N_META
WINDOW = 128
HEAD_DIM = 64
N_HEADS = 8
N_KV = 2
ATTN_W = N_HEADS * HEAD_DIM
KV_W = N_KV * HEAD_DIM
NUM_BUCKETS = 32
MAX_DISTANCE = 128
MASK_VALUE = -1e30
REC_HEADS = 4
REC_DK = 128
REC_DV = 128
REC_W = REC_HEADS * REC_DK
MIN_FORGET = 1e-30
D_FF = 2816
FFN_RESIDUAL = 0.5
EPS = 1e-6
N_NORMS = 6

FF_CHUNK = 256
N_FF_CHUNKS = D_FF // FF_CHUNK
CHUNK = 64
N_LEVELS = 6
SUB = 16
SUB_LEVEL = 4
SUB_DECAY_LIMIT = 60.0
IN_W = ATTN_W + 4 * KV_W + 5 * REC_W
VMEM_LIMIT = 52 * 1024 * 1024
ATTN_SUB = 3


_NT = (((1,), (1,)), ((), ()))


def _rms(x, g):
    ms = jnp.mean(x * x, axis=-1, keepdims=True)
    return x * lax.rsqrt(ms + EPS) * g


def _pick_tm(seq_len):
    for tm in range(704, 15, -16):
        if seq_len % tm == 0:
            return tm
    raise ValueError(f"no token tile for sequence length {seq_len}")


def _ffn_tail(x_ref, gpre_ref, gpost_ref, win_ref, wout_ref, o_ref, xn_ref, acc_ref):
    xn_ref[...] = _rms(x_ref[0], gpre_ref[...]).astype(BF16)
    for c in range(N_FF_CHUNKS):
        ab = jnp.dot(xn_ref[...], win_ref[c], preferred_element_type=F32)
        a = ab[:, :FF_CHUNK]
        b = ab[:, FF_CHUNK:]
        s = (a * jax.nn.sigmoid(a) * b).astype(BF16)
        y = jnp.dot(s, wout_ref[c], preferred_element_type=F32)
        if c == 0:
            acc_ref[...] = y
        else:
            acc_ref[...] += y
    o_ref[0] = x_ref[0] + FFN_RESIDUAL * _rms(acc_ref[...], gpost_ref[...])


def _ffn_kernel(x_ref, gpre_ref, gpost_ref, win_ref, wout_ref, o_ref, xn_ref, acc_ref):
    _ffn_tail(x_ref, gpre_ref, gpost_ref, win_ref, wout_ref, o_ref, xn_ref, acc_ref)


def _mix_ffn_kernel(x_ref, a_ref, r_ref, wo_ref, gmix_ref, gpre_ref, gpost_ref, win_ref, wout_ref, o_ref,
                    xn_ref, acc_ref):
    mix = (jnp.dot(a_ref[0], wo_ref[:ATTN_W, :], preferred_element_type=F32)
           + jnp.dot(r_ref[0], wo_ref[ATTN_W:, :], preferred_element_type=F32))
    o_ref[0] = x_ref[0] + _rms(mix, gmix_ref[...])
    _ffn_tail(o_ref, gpre_ref, gpost_ref, win_ref, wout_ref, o_ref, xn_ref, acc_ref)


def _ffn(h, gains, i_pre, win_c, wout_c, layer, mix=None):
    ns, L, _ = h.shape
    tm = _pick_tm(L)
    tok = pl.BlockSpec((1, tm, D_MODEL), lambda s, j: (s, j, 0))

    def gain(i):
        return pl.BlockSpec((None, 1, D_MODEL), lambda s, j: (layer * N_NORMS + i, 0, 0))

    def whole(shape):
        return pl.BlockSpec((None,) + shape, lambda s, j: (layer,) + (0,) * len(shape), pipeline_mode=pl.Buffered(1))

    ffn_specs = [gain(i_pre), gain(i_pre + 1), whole((N_FF_CHUNKS, D_MODEL, 2 * FF_CHUNK)),
                 whole((N_FF_CHUNKS, FF_CHUNK, D_MODEL))]
    if mix is None:
        body, in_specs, args = _ffn_kernel, [tok] + ffn_specs, (h, gains, gains, win_c, wout_c)
    else:
        attn, rec, wo_all, i_mix = mix
        half = pl.BlockSpec((1, tm, ATTN_W), lambda s, j: (s, j, 0))
        body = _mix_ffn_kernel
        in_specs = [tok, half, half, whole((ATTN_W + REC_W, D_MODEL)), gain(i_mix)] + ffn_specs
        args = (h, attn, rec, wo_all, gains, gains, gains, win_c, wout_c)
    return pl.pallas_call(
        body,
        grid=(ns, L // tm),
        in_specs=in_specs,
        out_specs=tok,
        out_shape=jax.ShapeDtypeStruct(h.shape, F32),
        scratch_shapes=[pltpu.VMEM((tm, D_MODEL), BF16), pltpu.VMEM((tm, D_MODEL), F32)],
        input_output_aliases={0: 0},
        compiler_params=pltpu.CompilerParams(
            dimension_semantics=("arbitrary", "arbitrary"), vmem_limit_bytes=VMEM_LIMIT),
        name="ffn" if mix is None else "mix_ffn",
    )(*args)


def _lower_bound(lbl, layer):
    m = jnp.max(lbl, axis=0, keepdims=True)
    e = jnp.exp(lbl - m)
    den = jnp.sum(e, axis=0, keepdims=True)
    if layer == 0:
        return jnp.zeros_like(den)
    return jnp.sum(e[1:layer + 1], axis=0, keepdims=True) / den


def _inproj_kernel(layer, tm, x_ref, g_ref, w_ref, lbl_ref,
                   q_ref, k_ref, v_ref, qr_ref, lff_ref, kf_ref, lfb_ref, kb_ref, ir_ref, sg_ref, u_ref):
    u_ref[...] = _rms(x_ref[0], g_ref[...]).astype(BF16)

    def proj(c0, n):
        return jnp.dot(u_ref[...], w_ref[:, c0:c0 + n], preferred_element_type=F32)

    rows = pl.program_id(1) * tm + lax.broadcasted_iota(jnp.int32, (tm, 1), 0)
    tok = rows >= PAD
    half = REC_W // 2
    plain = ((q_ref, 0), (q_ref, half), (k_ref, 0), (v_ref, 0))
    for step in range(4):
        d, part = divmod(step, 2)
        lf_ref, kk_ref = ((lff_ref, kf_ref), (lfb_ref, kb_ref))[d]
        zz = proj(step * REC_W, REC_W)
        z = zz[:, :half]
        e = jnp.exp(-jnp.abs(z))
        r = 1.0 / (1.0 + e)
        er = e * r
        pos = z >= 0.0
        sig_p = jnp.where(pos, r, er)
        sig_n = jnp.where(pos, er, r)
        cols = slice(part * half, (part + 1) * half)
        if layer == 0:
            f, kk = sig_p, sig_n
        else:
            lb = _lower_bound(lbl_ref[d], layer)[:, cols]
            f = lb + (1.0 - lb) * sig_p
            kk = (1.0 - lb) * sig_n
        lf_ref[0, :, cols] = jnp.where(tok, jnp.log(jnp.maximum(f, MIN_FORGET)), 0.0)
        kk_ref[0, :, cols] = jnp.where(tok, kk, 0.0)
        o_ref, o0 = plain[step]
        o_ref[0, :, o0:o0 + half] = zz[:, half:].astype(BF16)
    qr_ref[0] = proj(4 * REC_W, REC_W).astype(BF16)
    gz = proj(5 * REC_W, REC_W)
    sg_ref[0] = (gz * jax.nn.sigmoid(gz)).astype(BF16)
    ir_ref[0] = proj(6 * REC_W, REC_W).astype(BF16)


def _inproj(h, gains, i_gain, w_all, lb_logits, layer):
    ns, L, _ = h.shape
    tm = _pick_tm(L)
    depth = lb_logits.shape[1]

    def tok(width):
        return pl.BlockSpec((1, tm, width), lambda s, j: (s, j, 0))

    def out(width, dt):
        return jax.ShapeDtypeStruct((ns, L, width), dt)

    return pl.pallas_call(
        functools.partial(_inproj_kernel, layer, tm),
        grid=(ns, L // tm),
        in_specs=[tok(D_MODEL),
                  pl.BlockSpec((None, 1, D_MODEL), lambda s, j: (layer * N_NORMS + i_gain, 0, 0)),
                  pl.BlockSpec((None, D_MODEL, IN_W), lambda s, j: (layer, 0, 0), pipeline_mode=pl.Buffered(1)),
                  pl.BlockSpec((2, depth, REC_W), lambda s, j: (0, 0, 0))],
        out_specs=[tok(ATTN_W), tok(2 * KV_W), tok(2 * KV_W), tok(REC_W), tok(REC_W), tok(REC_W),
                   tok(REC_W), tok(REC_W), tok(REC_W), tok(REC_W)],
        out_shape=[out(ATTN_W, BF16), out(2 * KV_W, BF16), out(2 * KV_W, BF16), out(REC_W, BF16),
                   out(REC_W, F32), out(REC_W, F32), out(REC_W, F32), out(REC_W, F32),
                   out(REC_W, BF16), out(REC_W, BF16)],
        scratch_shapes=[pltpu.VMEM((tm, D_MODEL), BF16)],
        compiler_params=pltpu.CompilerParams(
            dimension_semantics=("arbitrary", "arbitrary"), vmem_limit_bytes=VMEM_LIMIT),
        name="inproj",
    )(h, gains, w_all, lb_logits)


def _t5_bucket(rel):
    half = NUM_BUCKETS // 2
    max_exact = half // 2
    ret = (rel > 0).astype(jnp.int32) * half
    n = jnp.abs(rel)
    nf = jnp.maximum(n, 1).astype(jnp.float32)
    large = max_exact + (jnp.log(nf / max_exact) / math.log(MAX_DISTANCE / max_exact)
                         * (half - max_exact)).astype(jnp.int32)
    large = jnp.clip(large, 0, half - 1)
    return ret + jnp.where(n < max_exact, n, large)


def _attn_bias(rel_bias, n_blocks):
    lp = n_blocks * BLOCK
    buckets, valids = [], []
    for b in (0, 1, 2, n_blocks - 1):
        q_pos = b * BLOCK + jnp.arange(BLOCK)
        meta_pos = jnp.arange(BLOCK)
        band_pos = (b - 1) * BLOCK + jnp.arange(3 * BLOCK)
        k_pos = jnp.concatenate([meta_pos, band_pos])
        rel = k_pos[None, :] - q_pos[:, None]
        band_ok = (band_pos >= BLOCK) & (band_pos < lp)
        valids.append(jnp.concatenate([
            jnp.broadcast_to((meta_pos >= PAD)[None, :], (BLOCK, BLOCK)),
            band_ok[None, :] & (jnp.abs(rel[:, BLOCK:]) <= WINDOW)], axis=1))
        buckets.append(_t5_bucket(rel))
    onehot = jax.nn.one_hot(jnp.stack(buckets), NUM_BUCKETS, dtype=F32)
    bias = jnp.einsum('cqkb,bh->chqk', onehot, rel_bias.astype(F32), precision=lax.Precision.HIGHEST)
    return jnp.where(jnp.stack(valids)[:, None], bias, MASK_VALUE)


def _attn_block(q_all, key_blocks, val_blocks, bias_ref, cls, sink_ref, layer, gain):
    group = N_HEADS // N_KV
    lo = lax.broadcasted_iota(jnp.int32, (BLOCK, 2 * HEAD_DIM), 1) < HEAD_DIM
    zero = jnp.zeros((BLOCK, 2 * HEAD_DIM), BF16)
    outs = []
    for g in range(N_KV):
        ls = slice(g * 2 * HEAD_DIM, (g + 1) * 2 * HEAD_DIM)
        keys = jnp.concatenate([kb[:, ls] for kb in key_blocks], axis=0)
        vals = jnp.concatenate([vb[:, ls] for vb in val_blocks], axis=0)
        qs = []
        for p in range(group // 2):
            q128 = q_all[:, (g * group + 2 * p) * HEAD_DIM:(g * group + 2 * p + 2) * HEAD_DIM]
            qs.append(jnp.where(lo, q128, zero))
            qs.append(jnp.where(lo, zero, q128))
        qst = jnp.concatenate(qs, axis=0)
        s = lax.dot_general(qst, keys, _NT, preferred_element_type=F32)
        es, invs = [], []
        for hh in range(group):
            head = g * group + hh
            b = bias_ref[cls, head]
            lg = jnp.where(b > 0.5 * MASK_VALUE, s[hh * BLOCK:(hh + 1) * BLOCK] + b, MASK_VALUE)
            sk = sink_ref[layer, head]
            m = jnp.maximum(jnp.max(lg, axis=-1, keepdims=True), sk)
            e = jnp.exp(lg - m)
            den = jnp.sum(e, axis=-1, keepdims=True) + jnp.exp(sk - m)
            es.append(e.astype(BF16))
            invs.append(1.0 / den)
        o = jnp.dot(jnp.concatenate(es, axis=0), vals, preferred_element_type=F32)
        for p in range(group // 2):
            oa = o[(2 * p) * BLOCK:(2 * p + 1) * BLOCK] * invs[2 * p]
            ob = o[(2 * p + 1) * BLOCK:(2 * p + 2) * BLOCK] * invs[2 * p + 1]
            outs.append(jnp.where(lo, oa, ob))
    return _rms(jnp.concatenate(outs, axis=1), gain).astype(BF16)


def _attn_kernel(nsub, nb, layer, q_ref, k0_ref, kp_ref, kc_ref, kn_ref, v0_ref, vp_ref, vc_ref, vn_ref,
                 bias_ref, sink_ref, gain_ref, o_ref):
    def band(prev_ref, own_ref, next_ref):
        blocks = [prev_ref[0]] + [own_ref[0, t * BLOCK:(t + 1) * BLOCK, :] for t in range(nsub)] + [next_ref[0]]
        return blocks

    kband = band(kp_ref, kc_ref, kn_ref)
    vband = band(vp_ref, vc_ref, vn_ref)
    for t in range(nsub):
        b = pl.program_id(1) * nsub + t
        cls = jnp.where(b == 0, 0, jnp.where(b == 1, 1, jnp.where(b == nb - 1, 3, 2)))
        rows = slice(t * BLOCK, (t + 1) * BLOCK)
        o_ref[0, rows, :] = _attn_block(q_ref[0, rows, :], [k0_ref[0]] + kband[t:t + 3], [v0_ref[0]] + vband[t:t + 3],
                                        bias_ref, cls, sink_ref, layer, gain_ref[...])


def _attention(q, k2, v2, bias, sink_all, gain_all, layer):
    ns, L, _ = q.shape
    nb = L // BLOCK
    nsub = ATTN_SUB if nb % ATTN_SUB == 0 else 1

    def kv_spec(rows, fn):
        return pl.BlockSpec((1, rows, 2 * KV_W), fn)

    kv_specs = [kv_spec(BLOCK, lambda s, j: (s, 0, 0)),
                kv_spec(BLOCK, lambda s, j: (s, jnp.maximum(j * nsub - 1, 0), 0)),
                kv_spec(nsub * BLOCK, lambda s, j: (s, j, 0)),
                kv_spec(BLOCK, lambda s, j: (s, jnp.minimum(j * nsub + nsub, nb - 1), 0))]
    tile = pl.BlockSpec((1, nsub * BLOCK, ATTN_W), lambda s, j: (s, j, 0))
    return pl.pallas_call(
        functools.partial(_attn_kernel, nsub, nb, layer),
        grid=(ns, nb // nsub),
        in_specs=[tile] + kv_specs + kv_specs + [
            pl.BlockSpec(bias.shape, lambda s, j: (0, 0, 0, 0), pipeline_mode=pl.Buffered(1)),
            pl.BlockSpec(memory_space=pltpu.SMEM),
            pl.BlockSpec((None, 1, ATTN_W), lambda s, j: (layer, 0, 0))],
        out_specs=tile,
        out_shape=jax.ShapeDtypeStruct((ns, L, ATTN_W), BF16),
        compiler_params=pltpu.CompilerParams(
            dimension_semantics=("arbitrary", "arbitrary"), vmem_limit_bytes=VMEM_LIMIT),
        name="attn",
    )(q, k2, k2, k2, k2, v2, v2, v2, v2, bias, sink_all, gain_all)


def _hgrn_consts():
    r = np.arange(CHUNK)[:, None]
    c = np.arange(CHUNK)[None, :]
    fwd = []
    for l in range(N_LEVELS):
        same = (r >> (l + 1)) == (c >> (l + 1))
        fwd.append(same & (((r >> l) & 1) == 1) & (((c >> l) & 1) == 0))
    fwd.append(r == c)
    fwd.append(((r // SUB) == (c // SUB)) & (c <= r))
    fwd = np.stack(fwd).astype(np.float32)
    masks = np.stack([fwd, np.transpose(fwd, (0, 2, 1))])
    tri = np.stack([(c <= r), (c >= r)]).astype(np.float32)
    return masks, tri


def _block_bcast(w, l, reverse, off):
    h = 1 << l
    if h >= 8:
        pieces = []
        for base in range(0, CHUNK, 2 * h):
            src = base + (h if reverse else h - 1)
            pieces.append(jnp.broadcast_to(w[src:src + 1, :], (2 * h, w.shape[1])))
        return jnp.concatenate(pieces, axis=0)
    o = off & (2 * h - 1)
    a = w
    s = 1
    while s < h:
        if reverse:
            a = jnp.where((o >= h + s) & (o < h + 2 * s), pltpu.roll(a, s, 0), a)
        else:
            a = jnp.where((o >= h - 2 * s) & (o < h - s), pltpu.roll(a, CHUNK - s, 0), a)
        s *= 2
    if reverse:
        return jnp.where(o < h, pltpu.roll(a, CHUNK - h, 0), a)
    return jnp.where(o >= h, pltpu.roll(a, h, 0), a)


def _pair_scores(qe, ke):
    return lax.dot_general(qe.astype(BF16), ke.astype(BF16), _NT, preferred_element_type=F32)


def _hgrn_finish(q, k, v, vt, w, scores, state_t, reverse):
    edge = w[0:1, :] if reverse else w[CHUNK - 1:CHUNK, :]
    intra = jnp.dot(scores.astype(BF16), v, preferred_element_type=F32)
    inter = lax.dot_general((q * jnp.exp(w)).astype(BF16), state_t.astype(BF16), _NT, preferred_element_type=F32)
    khat = (k * jnp.exp(edge - w)).astype(BF16)
    new_state = state_t * jnp.exp(edge) + jnp.dot(vt, khat, preferred_element_type=F32)
    return inter + intra, new_state


def _hgrn_chunk_any(q, k, g, v, vt, state_t, masks_ref, reverse):
    off = lax.broadcasted_iota(jnp.int32, (CHUNK, REC_DK), 0)
    w = g
    scores = masks_ref[N_LEVELS] * _pair_scores(q, k)
    for l in range(N_LEVELS):
        later = ((off >> l) & 1) == 1
        piv = _block_bcast(w, l, reverse, off)
        if reverse:
            expo = jnp.where(later, piv - w, w)
            w_next = w + jnp.where(later, 0.0, piv)
        else:
            expo = jnp.where(later, w, piv - w)
            w_next = w + jnp.where(later, piv, 0.0)
        e = jnp.exp(expo)
        scores = scores + masks_ref[l] * _pair_scores(q * e, k * e)
        w = w_next
    return _hgrn_finish(q, k, v, vt, w, scores, state_t, reverse)


def _hgrn_chunk_bounded(q, k, w, v, vt, state_t, masks_ref, reverse):
    scores = None
    for l in range(N_LEVELS - 1, SUB_LEVEL - 1, -1):
        e = jnp.exp(-jnp.abs(w - _block_bcast(w, l, reverse, None)))
        sl = masks_ref[l] * _pair_scores(q * e, k * e)
        scores = sl if scores is None else scores + sl
    zeros = jnp.zeros((SUB, w.shape[1]), F32)
    pieces = []
    for base in range(0, CHUNK, SUB):
        src = base + SUB if reverse else base - 1
        if 0 <= src < CHUNK:
            pieces.append(jnp.broadcast_to(w[src:src + 1, :], (SUB, w.shape[1])))
        else:
            pieces.append(zeros)
    d = w - jnp.concatenate(pieces, axis=0)
    sl = _pair_scores(q * jnp.exp(d), k * jnp.exp(-d))
    scores = scores + jnp.where(masks_ref[N_LEVELS + 1] > 0.5, sl, 0.0)
    return _hgrn_finish(q, k, v, vt, w, scores, state_t, reverse)


def _split_dot(tri, g):
    hi = g.astype(BF16)
    lo = (g - hi.astype(F32)).astype(BF16)
    return jnp.dot(tri, hi, preferred_element_type=F32) + jnp.dot(tri, lo, preferred_element_type=F32)


def _hgrn_kernel(n_chunks, q_ref, lff_ref, kf_ref, lfb_ref, kb_ref, v_ref, sg_ref, gain_ref, masks_ref, tri_ref,
                 o_ref, wf_ref, wb_ref, of_ref, ob_ref):
    def chunk_rows(c):
        return pl.ds(pl.multiple_of(c * CHUNK, CHUNK), CHUNK)

    def prepass(c, worst):
        rows = chunk_rows(c)
        wf = _split_dot(tri_ref[0], lff_ref[0, rows, :])
        wb = _split_dot(tri_ref[1], lfb_ref[0, rows, :])
        wf_ref[rows, :] = wf
        wb_ref[rows, :] = wb
        for i in range(CHUNK // SUB):
            last = (i + 1) * SUB - 1
            tf = wf[last:last + 1, :] - (wf[last - SUB:last - SUB + 1, :] if i > 0 else 0.0)
            first = i * SUB
            tb = wb[first:first + 1, :] - (wb[first + SUB:first + SUB + 1, :] if first + SUB < CHUNK else 0.0)
            worst = jnp.minimum(worst, jnp.minimum(tf, tb))
        return worst

    worst = lax.fori_loop(0, n_chunks, prepass, jnp.zeros((1, REC_DK), F32), unroll=math.gcd(n_chunks, 6))
    bounded = jnp.min(worst) >= -SUB_DECAY_LIMIT

    def load(c):
        rows = chunk_rows(c)
        q = q_ref[0, rows, :].astype(F32)
        v = v_ref[0, rows, :]
        vt = v.astype(F32).T.astype(BF16)
        return rows, q, v, vt

    def sweep(step, unroll):
        def body(i, carry):
            sf, sb = carry
            rows, q, v, vt = load(i)
            o, sf = step(rows, q, v, vt, sf, False)
            of_ref[rows, :] = o
            rows, q, v, vt = load(n_chunks - 1 - i)
            o, sb = step(rows, q, v, vt, sb, True)
            ob_ref[rows, :] = o
            return sf, sb

        zero = jnp.zeros((REC_DV, REC_DK), F32)
        lax.fori_loop(0, n_chunks, body, (zero, zero), unroll=math.gcd(n_chunks, unroll))

    @pl.when(bounded)
    def _():
        def step(rows, q, v, vt, st, reverse):
            k_ref, w_ref = (kb_ref, wb_ref) if reverse else (kf_ref, wf_ref)
            return _hgrn_chunk_bounded(q, k_ref[0, rows, :], w_ref[rows, :], v, vt, st,
                                       masks_ref.at[1 if reverse else 0], reverse)
        sweep(step, 6)

    @pl.when(jnp.logical_not(bounded))
    def _():
        def step(rows, q, v, vt, st, reverse):
            k_ref, g_ref = (kb_ref, lfb_ref) if reverse else (kf_ref, lff_ref)
            return _hgrn_chunk_any(q, k_ref[0, rows, :], g_ref[0, rows, :], v, vt, st,
                                   masks_ref.at[1 if reverse else 0], reverse)
        sweep(step, 2)

    def epilogue(t, carry):
        rows = pl.ds(pl.multiple_of(t * BLOCK, BLOCK), BLOCK)
        o = of_ref[rows, :] + ob_ref[rows, :]
        o = _rms(o, gain_ref[...]) * sg_ref[0, rows, :].astype(F32)
        o_ref[0, rows, :] = o.astype(BF16)
        return carry

    n_tiles = n_chunks * CHUNK // BLOCK
    lax.fori_loop(0, n_tiles, epilogue, 0, unroll=math.gcd(n_tiles, 3))


def _hgrn(qr, lff, kf, lfb, kb, ir, sg, gain_all, layer, masks, tri):
    ns, L, _ = qr.shape
    head = pl.BlockSpec((1, L, REC_DK), lambda s, hd: (s, 0, hd))
    scratch = pltpu.VMEM((L, REC_DV), F32)
    return pl.pallas_call(
        functools.partial(_hgrn_kernel, L // CHUNK),
        grid=(ns, REC_HEADS),
        in_specs=[head, head, head, head, head, head, head,
                  pl.BlockSpec((None, 1, REC_DV), lambda s, hd: (layer, 0, hd)),
                  pl.BlockSpec(masks.shape, lambda s, hd: (0, 0, 0, 0)),
                  pl.BlockSpec(tri.shape, lambda s, hd: (0, 0, 0))],
        out_specs=head,
        out_shape=jax.ShapeDtypeStruct((ns, L, REC_W), BF16),
        scratch_shapes=[scratch, scratch, scratch, scratch],
        compiler_params=pltpu.CompilerParams(
            dimension_semantics=("arbitrary", "arbitrary"), vmem_limit_bytes=VMEM_LIMIT),
        name="hgrn",
    )(qr, lff, kf, lfb, kb, ir, sg, gain_all, masks, tri)


def _prep_ffn(w_in, w_out):
    depth = w_in.shape[0]
    a = w_in[:, :, :D_FF].reshape(depth, D_MODEL, N_FF_CHUNKS, FF_CHUNK)
    b = w_in[:, :, D_FF:].reshape(depth, D_MODEL, N_FF_CHUNKS, FF_CHUNK)
    win_c = jnp.transpose(jnp.concatenate([a, b], axis=3), (0, 2, 1, 3)).astype(BF16)
    wout_c = w_out.reshape(depth, N_FF_CHUNKS, FF_CHUNK, D_MODEL).astype(BF16)
    return win_c, wout_c


def _prep_in(w):
    depth = w.shape[0]
    half = REC_W // 2
    q = w[:, :, :ATTN_W] * (HEAD_DIM ** -0.5)
    k = w[:, :, ATTN_W:ATTN_W + KV_W]
    v = w[:, :, ATTN_W + KV_W:ATTN_W + 2 * KV_W]
    c0 = ATTN_W + 2 * KV_W
    qr, zf, zb, ir, gr = (w[:, :, c0 + i * REC_W:c0 + (i + 1) * REC_W] for i in range(5))

    def dup(t):
        t = t.reshape(depth, D_MODEL, N_KV, 1, HEAD_DIM)
        return jnp.broadcast_to(t, (depth, D_MODEL, N_KV, 2, HEAD_DIM)).reshape(depth, D_MODEL, 2 * KV_W)

    return jnp.concatenate([zf[..., :half], q[..., :half], zf[..., half:], q[..., half:], zb[..., :half], dup(k),
                            zb[..., half:], dup(v), qr, gr, ir], axis=2).astype(BF16)


def kernel(x_prompt, x_sample, meta_tokens, rel_bias, lb_logits, norm_gains, w_in, attn_sink, attn_out_gain,
           rec_out_gain, w_out, ffn1_w_in, ffn1_w_out, ffn2_w_in, ffn2_w_out):
    depth = w_in.shape[0]
    n_prompt = x_prompt.shape[0]
    x = jnp.concatenate([x_prompt, x_sample], axis=0)
    ns, seq, _ = x.shape
    h = jnp.concatenate([jnp.zeros((ns, PAD, D_MODEL), x.dtype),
                         jnp.broadcast_to(meta_tokens.astype(x.dtype)[None], (ns, N_META, D_MODEL)),
                         x], axis=1)
    n_blocks = h.shape[1] // BLOCK
    bias = _attn_bias(rel_bias, n_blocks)
    masks_np, tri_np = _hgrn_consts()
    masks = jnp.asarray(masks_np)
    tri = jnp.asarray(tri_np, dtype=BF16)
    lbl = lb_logits.astype(F32)
    gains = norm_gains.astype(F32).reshape(depth * N_NORMS, 1, D_MODEL)
    ffn1 = _prep_ffn(ffn1_w_in, ffn1_w_out)
    ffn2 = _prep_ffn(ffn2_w_in, ffn2_w_out)
    w_in_all = _prep_in(w_in)
    w_out_all = w_out.astype(BF16)
    sink = attn_sink.astype(F32)
    attn_gain = attn_out_gain.astype(F32).reshape(depth, 1, ATTN_W)
    rec_gain = rec_out_gain.astype(F32).reshape(depth, 1, REC_W)
    for l in range(depth):
        h = _ffn(h, gains, 0, *ffn1, l)
        qa, k2, v2, qr, lff, kf, lfb, kb, ir, sg = _inproj(h, gains, 2, w_in_all, lbl, l)
        attn = _attention(qa, k2, v2, bias, sink, attn_gain, l)
        rec = _hgrn(qr, lff, kf, lfb, kb, ir, sg, rec_gain, l, masks, tri)
        h = _ffn(h, gains, 4, *ffn2, l, mix=(attn, rec, w_out_all, 3))
    y = h[:, PAD + N_META:]
    return (y[:n_prompt], y[n_prompt:])
```

```python
import functools
import math

import numpy as np
import jax
import jax.numpy as jnp
from jax import lax
from jax.experimental import pallas as pl
from jax.experimental.pallas import tpu as pltpu

F32 = jnp.float32
BF16 = jnp.bfloat16

D_MODEL = 1024
N_META = 16
BLOCK = 128
PAD = BLOCK - N_META
WINDOW = 128
HEAD_DIM = 64
N_HEADS = 8
N_KV = 2
ATTN_W = N_HEADS * HEAD_DIM
KV_W = N_KV * HEAD_DIM
NUM_BUCKETS = 32
MAX_DISTANCE = 128
MASK_VALUE = -1e30
REC_HEADS = 4
REC_DK = 128
REC_DV = 128
REC_W = REC_HEADS * REC_DK
MIN_FORGET = 1e-30
D_FF = 2816
FFN_RESIDUAL = 0.5
EPS = 1e-6
LOG2E = math.log2(math.e)
N_NORMS = 6

FF_CHUNK = 256
N_FF_CHUNKS = D_FF // FF_CHUNK
CHUNK = 64
N_LEVELS = 6
SUB = 32
SUB_LEVEL = 5
SUB_DECAY_LIMIT = 60.0
IN_W = ATTN_W + 4 * KV_W + 5 * REC_W
VMEM_LIMIT = 52 * 1024 * 1024
FINAL_TM = 512
ATTN_SUB = 3


_NT = (((1,), (1,)), ((), ()))


def _rms(x, g):
    ms = jnp.mean(x * x, axis=-1, keepdims=True)
    return x * lax.rsqrt(ms + EPS) * g


def _pick_tm(seq_len):
    for tm in range(704, 15, -16):
        if seq_len % tm == 0:
            return tm
    raise ValueError(f"no token tile for sequence length {seq_len}")


def _ffn_tail(x_ref, gpre_ref, gpost_ref, win_ref, wout_ref, o_ref, xn_ref, acc_ref):
    xn_ref[...] = _rms(x_ref[...], gpre_ref[...]).astype(BF16)
    for c in range(N_FF_CHUNKS):
        ab = jnp.dot(xn_ref[...], win_ref[c], preferred_element_type=F32)
        a = ab[:, :FF_CHUNK]
        b = ab[:, FF_CHUNK:]
        s = (a * jax.nn.sigmoid(a) * b).astype(BF16)
        y = jnp.dot(s, wout_ref[c], preferred_element_type=F32)
        if c == 0:
            acc_ref[...] = y
        else:
            acc_ref[...] += y
    o_ref[...] = x_ref[...] + FFN_RESIDUAL * _rms(acc_ref[...], gpost_ref[...])


def _ffn_kernel(x_ref, gpre_ref, gpost_ref, win_ref, wout_ref, o_ref, xn_ref, acc_ref):
    _ffn_tail(x_ref, gpre_ref, gpost_ref, win_ref, wout_ref, o_ref, xn_ref, acc_ref)


def _ffn_into_kernel(dst_ref, x_ref, gpre_ref, gpost_ref, win_ref, wout_ref, o_ref, xn_ref, acc_ref):
    del dst_ref
    _ffn_tail(x_ref, gpre_ref, gpost_ref, win_ref, wout_ref, o_ref, xn_ref, acc_ref)


def _mix_ffn_kernel(x_ref, a_ref, r_ref, wo_ref, gmix_ref, gpre_ref, gpost_ref, win_ref, wout_ref, o_ref,
                    xn_ref, acc_ref):
    mix = (jnp.dot(a_ref[...], wo_ref[:ATTN_W, :], preferred_element_type=F32)
           + jnp.dot(r_ref[...], wo_ref[ATTN_W:, :], preferred_element_type=F32))
    o_ref[...] = x_ref[...] + _rms(mix, gmix_ref[...])
    _ffn_tail(o_ref, gpre_ref, gpost_ref, win_ref, wout_ref, o_ref, xn_ref, acc_ref)


def _ffn(h, gains, i_pre, win_c, wout_c, layer, mix=None, final=None):
    ns, L, _ = h.shape
    if final is None:
        tm = _pick_tm(L)
        seq0, n_seq, n_tiles = 0, ns, L // tm

        def tok_spec(width):
            return pl.BlockSpec((None, tm, width), lambda s, j: (s, j, 0))

        out_spec = tok_spec(D_MODEL)
        out_shape = jax.ShapeDtypeStruct(h.shape, F32)
        aliases = {0: 0}
    else:
        seq0, n_seq = final
        tm = math.gcd(L - BLOCK, FINAL_TM)
        n_tiles = (L - BLOCK) // tm

        def tok_spec(width):
            return pl.BlockSpec((pl.Element(tm), pl.Element(width)),
                                lambda s, j: (pl.multiple_of((s + seq0) * L + BLOCK + j * tm, BLOCK), 0))

        out_spec = pl.BlockSpec((tm, D_MODEL), lambda s, j: (s * n_tiles + j, 0))
        out_shape = jax.ShapeDtypeStruct((n_seq * (L - BLOCK), D_MODEL), F32)
        aliases = {}
    tok = tok_spec(D_MODEL)

    def gain(i):
        return pl.BlockSpec((None, 1, D_MODEL), lambda s, j: (layer * N_NORMS + i, 0, 0))

    def whole(shape):
        return pl.BlockSpec((None,) + shape, lambda s, j: (layer,) + (0,) * len(shape), pipeline_mode=pl.Buffered(1))

    ffn_specs = [gain(i_pre), gain(i_pre + 1), whole((N_FF_CHUNKS, D_MODEL, 2 * FF_CHUNK)),
                 whole((N_FF_CHUNKS, FF_CHUNK, D_MODEL))]
    if mix is None:
        body, in_specs, args = _ffn_kernel, [tok] + ffn_specs, (h, gains, gains, win_c, wout_c)
    else:
        attn, rec, wo_all, i_mix = mix
        if final is not None:
            h, attn, rec = (t.reshape(ns * L, t.shape[-1]) for t in (h, attn, rec))
        half = tok_spec(ATTN_W)
        body = _mix_ffn_kernel
        in_specs = [tok, half, half, whole((ATTN_W + REC_W, D_MODEL)), gain(i_mix)] + ffn_specs
        args = (h, attn, rec, wo_all, gains, gains, gains, win_c, wout_c)
    return pl.pallas_call(
        body,
        grid=(n_seq, n_tiles),
        in_specs=in_specs,
        out_specs=out_spec,
        out_shape=out_shape,
        scratch_shapes=[pltpu.VMEM((tm, D_MODEL), BF16), pltpu.VMEM((tm, D_MODEL), F32)],
        input_output_aliases=aliases,
        compiler_params=pltpu.CompilerParams(
            dimension_semantics=("arbitrary", "arbitrary"), vmem_limit_bytes=VMEM_LIMIT),
        name="ffn" if mix is None else "mix_ffn",
    )(*args)


def _ffn_embed(x_parts, meta_tokens, gains, win_c, wout_c, L):
    seq = x_parts[0].shape[1]
    ns = sum(x.shape[0] for x in x_parts)
    tm = math.gcd(seq, FINAL_TM)
    head = jnp.concatenate([jnp.zeros((PAD, D_MODEL), F32), meta_tokens.astype(F32)], axis=0)
    gain = [pl.BlockSpec((None, 1, D_MODEL), lambda s, j, i=i: (i, 0, 0)) for i in (0, 1)]
    weights = [pl.BlockSpec((None,) + w.shape[1:], lambda s, j, n=w.ndim: (0,) * n, pipeline_mode=pl.Buffered(1))
               for w in (win_c, wout_c)]

    def call(src, src_spec, grid, rows, row0, dst):
        body = _ffn_kernel if dst is None else _ffn_into_kernel
        extra = [] if dst is None else [pl.BlockSpec(memory_space=pl.ANY)]
        args = (src, gains, gains, win_c, wout_c) if dst is None else (dst, src, gains, gains, win_c, wout_c)
        return pl.pallas_call(
            body,
            grid=grid,
            in_specs=extra + [src_spec] + gain + weights,
            out_specs=pl.BlockSpec((pl.Element(rows), pl.Element(D_MODEL)),
                                   lambda s, j: (pl.multiple_of(row0(s, j), BLOCK), 0)),
            out_shape=jax.ShapeDtypeStruct((ns * L, D_MODEL), F32),
            scratch_shapes=[pltpu.VMEM((rows, D_MODEL), BF16), pltpu.VMEM((rows, D_MODEL), F32)],
            input_output_aliases={} if dst is None else {0: 0},
            compiler_params=pltpu.CompilerParams(
                dimension_semantics=("arbitrary", "arbitrary"), vmem_limit_bytes=VMEM_LIMIT),
            name="ffn_embed",
        )(*args)

    h, seq0 = None, 0
    for x in x_parts:
        h = call(x, pl.BlockSpec((None, tm, D_MODEL), lambda s, j: (s, j, 0)), (x.shape[0], seq // tm), tm,
                 lambda s, j, seq0=seq0: (s + seq0) * L + BLOCK + j * tm, h)
        seq0 += x.shape[0]
    h = call(head, pl.BlockSpec((BLOCK, D_MODEL), lambda s, j: (0, 0)), (ns, 1), BLOCK, lambda s, j: s * L, h)
    return h.reshape(ns, L, D_MODEL)


def _lower_bound(lbl, layer):
    m = jnp.max(lbl, axis=0, keepdims=True)
    e = jnp.exp(lbl - m)
    den = jnp.sum(e, axis=0, keepdims=True)
    if layer == 0:
        return jnp.zeros_like(den)
    return jnp.sum(e[1:layer + 1], axis=0, keepdims=True) / den


def _inproj_kernel(layer, tm, x_ref, g_ref, w_ref, lbl_ref,
                   q_ref, k_ref, v_ref, qr_ref, lff_ref, kf_ref, lfb_ref, kb_ref, ir_ref, sg_ref, u_ref):
    u_ref[...] = _rms(x_ref[0], g_ref[...]).astype(BF16)

    def proj(c0, n):
        return jnp.dot(u_ref[...], w_ref[:, c0:c0 + n], preferred_element_type=F32)

    rows = pl.program_id(1) * tm + lax.broadcasted_iota(jnp.int32, (tm, 1), 0)
    tok = rows >= PAD
    half = REC_W // 2
    plain = ((q_ref, 0), (q_ref, half), (k_ref, 0), (v_ref, 0))
    for step in range(4):
        d, part = divmod(step, 2)
        lf_ref, kk_ref = ((lff_ref, kf_ref), (lfb_ref, kb_ref))[d]
        zz = proj(step * REC_W, REC_W)
        z = zz[:, :half]
        e = jnp.exp(-jnp.abs(z))
        r = 1.0 / (1.0 + e)
        er = e * r
        pos = z >= 0.0
        sig_p = jnp.where(pos, r, er)
        sig_n = jnp.where(pos, er, r)
        cols = slice(part * half, (part + 1) * half)
        if layer == 0:
            f, kk = sig_p, sig_n
        else:
            lb = _lower_bound(lbl_ref[d], layer)[:, cols]
            f = lb + (1.0 - lb) * sig_p
            kk = (1.0 - lb) * sig_n
        lf_ref[0, :, cols] = jnp.where(tok, jnp.log(jnp.maximum(f, MIN_FORGET)), 0.0)
        kk_ref[0, :, cols] = jnp.where(tok, kk, 0.0)
        o_ref, o0 = plain[step]
        o_ref[0, :, o0:o0 + half] = zz[:, half:].astype(BF16)
    qr_ref[0] = proj(4 * REC_W, REC_W).astype(BF16)
    gz = proj(5 * REC_W, REC_W)
    sg_ref[0] = (gz * jax.nn.sigmoid(gz)).astype(BF16)
    ir_ref[0] = proj(6 * REC_W, REC_W).astype(BF16)


def _inproj(h, gains, i_gain, w_all, lb_logits, layer):
    ns, L, _ = h.shape
    tm = _pick_tm(L)
    depth = lb_logits.shape[1]

    def tok(width):
        return pl.BlockSpec((1, tm, width), lambda s, j: (s, j, 0))

    def out(width, dt):
        return jax.ShapeDtypeStruct((ns, L, width), dt)

    return pl.pallas_call(
        functools.partial(_inproj_kernel, layer, tm),
        grid=(ns, L // tm),
        in_specs=[tok(D_MODEL),
                  pl.BlockSpec((None, 1, D_MODEL), lambda s, j: (layer * N_NORMS + i_gain, 0, 0)),
                  pl.BlockSpec((None, D_MODEL, IN_W), lambda s, j: (layer, 0, 0), pipeline_mode=pl.Buffered(1)),
                  pl.BlockSpec((2, depth, REC_W), lambda s, j: (0, 0, 0))],
        out_specs=[tok(ATTN_W), tok(2 * KV_W), tok(2 * KV_W), tok(REC_W), tok(REC_W), tok(REC_W),
                   tok(REC_W), tok(REC_W), tok(REC_W), tok(REC_W)],
        out_shape=[out(ATTN_W, BF16), out(2 * KV_W, BF16), out(2 * KV_W, BF16), out(REC_W, BF16),
                   out(REC_W, F32), out(REC_W, F32), out(REC_W, F32), out(REC_W, F32),
                   out(REC_W, BF16), out(REC_W, BF16)],
        scratch_shapes=[pltpu.VMEM((tm, D_MODEL), BF16)],
        compiler_params=pltpu.CompilerParams(
            dimension_semantics=("arbitrary", "arbitrary"), vmem_limit_bytes=VMEM_LIMIT),
        name="inproj",
    )(h, gains, w_all, lb_logits)


def _t5_bucket(rel):
    half = NUM_BUCKETS // 2
    max_exact = half // 2
    ret = (rel > 0).astype(jnp.int32) * half
    n = jnp.abs(rel)
    nf = jnp.maximum(n, 1).astype(jnp.float32)
    large = max_exact + (jnp.log(nf / max_exact) / math.log(MAX_DISTANCE / max_exact)
                         * (half - max_exact)).astype(jnp.int32)
    large = jnp.clip(large, 0, half - 1)
    return ret + jnp.where(n < max_exact, n, large)


def _attn_bias(rel_bias, n_blocks):
    lp = n_blocks * BLOCK
    buckets, valids = [], []
    for b in (0, 1, 2, n_blocks - 1):
        q_pos = b * BLOCK + jnp.arange(BLOCK)
        meta_pos = jnp.arange(BLOCK)
        band_pos = (b - 1) * BLOCK + jnp.arange(3 * BLOCK)
        k_pos = jnp.concatenate([meta_pos, band_pos])
        rel = k_pos[None, :] - q_pos[:, None]
        band_ok = (band_pos >= BLOCK) & (band_pos < lp)
        valids.append(jnp.concatenate([
            jnp.broadcast_to((meta_pos >= PAD)[None, :], (BLOCK, BLOCK)),
            band_ok[None, :] & (jnp.abs(rel[:, BLOCK:]) <= WINDOW)], axis=1))
        buckets.append(_t5_bucket(rel))
    onehot = jax.nn.one_hot(jnp.stack(buckets), NUM_BUCKETS, dtype=F32)
    bias = jnp.einsum('cqkb,bh->chqk', onehot, rel_bias.astype(F32), precision=lax.Precision.HIGHEST)
    return jnp.where(jnp.stack(valids)[:, None], bias * LOG2E, MASK_VALUE)


def _attn_block(q_all, key_blocks, val_blocks, bias_ref, cls, sink_ref, layer, gain):
    group = N_HEADS // N_KV
    lo = lax.broadcasted_iota(jnp.int32, (BLOCK, 2 * HEAD_DIM), 1) < HEAD_DIM
    zero = jnp.zeros((BLOCK, 2 * HEAD_DIM), BF16)
    outs = []
    for g in range(N_KV):
        ls = slice(g * 2 * HEAD_DIM, (g + 1) * 2 * HEAD_DIM)
        keys = jnp.concatenate([kb[:, ls] for kb in key_blocks], axis=0)
        vals = jnp.concatenate([vb[:, ls] for vb in val_blocks], axis=0)
        qs = []
        for p in range(group // 2):
            q128 = q_all[:, (g * group + 2 * p) * HEAD_DIM:(g * group + 2 * p + 2) * HEAD_DIM]
            qs.append(jnp.where(lo, q128, zero))
            qs.append(jnp.where(lo, zero, q128))
        qst = jnp.concatenate(qs, axis=0)
        s = lax.dot_general(qst, keys, _NT, preferred_element_type=F32)
        es, invs = [], []
        for hh in range(group):
            head = g * group + hh
            b = bias_ref[cls, head]
            lg = jnp.where(b > 0.5 * MASK_VALUE, s[hh * BLOCK:(hh + 1) * BLOCK] + b, MASK_VALUE)
            sk = sink_ref[layer, head]
            m = jnp.maximum(jnp.max(lg, axis=-1, keepdims=True), sk)
            e = jnp.exp2(lg - m)
            den = jnp.sum(e, axis=-1, keepdims=True) + jnp.exp2(sk - m)
            es.append(e.astype(BF16))
            invs.append(1.0 / den)
        o = jnp.dot(jnp.concatenate(es, axis=0), vals, preferred_element_type=F32)
        for p in range(group // 2):
            oa = o[(2 * p) * BLOCK:(2 * p + 1) * BLOCK] * invs[2 * p]
            ob = o[(2 * p + 1) * BLOCK:(2 * p + 2) * BLOCK] * invs[2 * p + 1]
            outs.append(jnp.where(lo, oa, ob))
    return _rms(jnp.concatenate(outs, axis=1), gain).astype(BF16)


def _attn_kernel(nsub, nb, layer, q_ref, k0_ref, kp_ref, kc_ref, kn_ref, v0_ref, vp_ref, vc_ref, vn_ref,
                 bias_ref, sink_ref, gain_ref, o_ref):
    def band(prev_ref, own_ref, next_ref):
        blocks = [prev_ref[0]] + [own_ref[0, t * BLOCK:(t + 1) * BLOCK, :] for t in range(nsub)] + [next_ref[0]]
        return blocks

    kband = band(kp_ref, kc_ref, kn_ref)
    vband = band(vp_ref, vc_ref, vn_ref)
    for t in range(nsub):
        b = pl.program_id(1) * nsub + t
        cls = jnp.where(b == 0, 0, jnp.where(b == 1, 1, jnp.where(b == nb - 1, 3, 2)))
        rows = slice(t * BLOCK, (t + 1) * BLOCK)
        o_ref[0, rows, :] = _attn_block(q_ref[0, rows, :], [k0_ref[0]] + kband[t:t + 3], [v0_ref[0]] + vband[t:t + 3],
                                        bias_ref, cls, sink_ref, layer, gain_ref[...])


def _attention(q, k2, v2, bias, sink_all, gain_all, layer):
    ns, L, _ = q.shape
    nb = L // BLOCK
    nsub = ATTN_SUB if nb % ATTN_SUB == 0 else 1

    def kv_spec(rows, fn):
        return pl.BlockSpec((1, rows, 2 * KV_W), fn)

    kv_specs = [kv_spec(BLOCK, lambda s, j: (s, 0, 0)),
                kv_spec(BLOCK, lambda s, j: (s, jnp.maximum(j * nsub - 1, 0), 0)),
                kv_spec(nsub * BLOCK, lambda s, j: (s, j, 0)),
                kv_spec(BLOCK, lambda s, j: (s, jnp.minimum(j * nsub + nsub, nb - 1), 0))]
    tile = pl.BlockSpec((1, nsub * BLOCK, ATTN_W), lambda s, j: (s, j, 0))
    return pl.pallas_call(
        functools.partial(_attn_kernel, nsub, nb, layer),
        grid=(ns, nb // nsub),
        in_specs=[tile] + kv_specs + kv_specs + [
            pl.BlockSpec(bias.shape, lambda s, j: (0, 0, 0, 0), pipeline_mode=pl.Buffered(1)),
            pl.BlockSpec(memory_space=pltpu.SMEM),
            pl.BlockSpec((None, 1, ATTN_W), lambda s, j: (layer, 0, 0))],
        out_specs=tile,
        out_shape=jax.ShapeDtypeStruct((ns, L, ATTN_W), BF16),
        compiler_params=pltpu.CompilerParams(
            dimension_semantics=("arbitrary", "arbitrary"), vmem_limit_bytes=VMEM_LIMIT),
        name="attn",
    )(q, k2, k2, k2, k2, v2, v2, v2, v2, bias, sink_all, gain_all)


def _hgrn_consts():
    r = np.arange(CHUNK)[:, None]
    c = np.arange(CHUNK)[None, :]
    fwd = []
    for l in range(N_LEVELS):
        same = (r >> (l + 1)) == (c >> (l + 1))
        fwd.append(same & (((r >> l) & 1) == 1) & (((c >> l) & 1) == 0))
    fwd.append(r == c)
    fwd.append(((r // SUB) == (c // SUB)) & (c <= r))
    fwd = np.stack(fwd).astype(np.float32)
    masks = np.stack([fwd, np.transpose(fwd, (0, 2, 1))])
    tri = np.stack([(c <= r), (c >= r)]).astype(np.float32)
    return masks, tri


def _block_bcast(w, l, reverse, off):
    h = 1 << l
    if h >= 8:
        pieces = []
        for base in range(0, CHUNK, 2 * h):
            src = base + (h if reverse else h - 1)
            pieces.append(jnp.broadcast_to(w[src:src + 1, :], (2 * h, w.shape[1])))
        return jnp.concatenate(pieces, axis=0)
    o = off & (2 * h - 1)
    a = w
    s = 1
    while s < h:
        if reverse:
            a = jnp.where((o >= h + s) & (o < h + 2 * s), pltpu.roll(a, s, 0), a)
        else:
            a = jnp.where((o >= h - 2 * s) & (o < h - s), pltpu.roll(a, CHUNK - s, 0), a)
        s *= 2
    if reverse:
        return jnp.where(o < h, pltpu.roll(a, CHUNK - h, 0), a)
    return jnp.where(o >= h, pltpu.roll(a, h, 0), a)


def _pair_scores(qe, ke):
    return lax.dot_general(qe.astype(BF16), ke.astype(BF16), _NT, preferred_element_type=F32)


def _hgrn_finish(q, k, v, w, scores, state_t, reverse):
    edge = w[0:1, :] if reverse else w[CHUNK - 1:CHUNK, :]
    intra = jnp.dot(scores.astype(BF16), v, preferred_element_type=F32)
    inter = lax.dot_general((q * jnp.exp(w)).astype(BF16), state_t.astype(BF16), _NT, preferred_element_type=F32)
    khat = (k * jnp.exp(edge - w)).astype(BF16)
    new_state = state_t * jnp.exp(edge) + lax.dot_general(v, khat, (((0,), (0,)), ((), ())), preferred_element_type=F32)
    return inter + intra, new_state


def _hgrn_chunk_any(q, k, g, v, state_t, masks_ref, reverse):
    off = lax.broadcasted_iota(jnp.int32, (CHUNK, REC_DK), 0)
    w = g
    scores = masks_ref[N_LEVELS] * _pair_scores(q, k)
    for l in range(N_LEVELS):
        later = ((off >> l) & 1) == 1
        piv = _block_bcast(w, l, reverse, off)
        if reverse:
            expo = jnp.where(later, piv - w, w)
            w_next = w + jnp.where(later, 0.0, piv)
        else:
            expo = jnp.where(later, w, piv - w)
            w_next = w + jnp.where(later, piv, 0.0)
        e = jnp.exp(expo)
        scores = scores + masks_ref[l] * _pair_scores(q * e, k * e)
        w = w_next
    return _hgrn_finish(q, k, v, w, scores, state_t, reverse)


def _hgrn_chunk_bounded(q, k, w, v, state_t, masks_ref, reverse):
    scores = None
    for l in range(N_LEVELS - 1, SUB_LEVEL - 1, -1):
        e = jnp.exp(-jnp.abs(w - _block_bcast(w, l, reverse, None)))
        sl = masks_ref[l] * _pair_scores(q * e, k * e)
        scores = sl if scores is None else scores + sl
    zeros = jnp.zeros((SUB, w.shape[1]), F32)
    pieces = []
    for base in range(0, CHUNK, SUB):
        src = base + SUB if reverse else base - 1
        if 0 <= src < CHUNK:
            pieces.append(jnp.broadcast_to(w[src:src + 1, :], (SUB, w.shape[1])))
        else:
            pieces.append(zeros)
    d = w - jnp.concatenate(pieces, axis=0)
    sl = _pair_scores(q * jnp.exp(d), k * jnp.exp(-d))
    scores = scores + jnp.where(masks_ref[N_LEVELS + 1] > 0.5, sl, 0.0)
    return _hgrn_finish(q, k, v, w, scores, state_t, reverse)


def _split_dot(tri, g):
    hi = g.astype(BF16)
    lo = (g - hi.astype(F32)).astype(BF16)
    both = jnp.dot(tri, jnp.concatenate([hi, lo], axis=1), preferred_element_type=F32)
    return both[:, :g.shape[1]] + both[:, g.shape[1]:]


def _hgrn_kernel(n_chunks, q_ref, lff_ref, kf_ref, lfb_ref, kb_ref, v_ref, sg_ref, gain_ref, masks_ref, tri_ref,
                 o_ref, wf_ref, wb_ref, of_ref, ob_ref):
    def chunk_rows(c):
        return pl.ds(pl.multiple_of(c * CHUNK, CHUNK), CHUNK)

    def prepass(c, worst):
        rows = chunk_rows(c)
        wf = _split_dot(tri_ref[0], lff_ref[0, rows, :])
        wb = _split_dot(tri_ref[1], lfb_ref[0, rows, :])
        wf_ref[rows, :] = wf
        wb_ref[rows, :] = wb
        for i in range(CHUNK // SUB):
            last = (i + 1) * SUB - 1
            tf = wf[last:last + 1, :] - (wf[last - SUB:last - SUB + 1, :] if i > 0 else 0.0)
            first = i * SUB
            tb = wb[first:first + 1, :] - (wb[first + SUB:first + SUB + 1, :] if first + SUB < CHUNK else 0.0)
            worst = jnp.minimum(worst, jnp.minimum(tf, tb))
        return worst

    worst = lax.fori_loop(0, n_chunks, prepass, jnp.zeros((1, REC_DK), F32), unroll=math.gcd(n_chunks, 6))
    bounded = jnp.min(worst) >= -SUB_DECAY_LIMIT

    def load(c):
        rows = chunk_rows(c)
        q = q_ref[0, rows, :].astype(F32)
        v = v_ref[0, rows, :]
        return rows, q, v

    def sweep(step, unroll):
        def body(i, carry):
            sf, sb = carry
            rows, q, v = load(i)
            o, sf = step(rows, q, v, sf, False)
            of_ref[rows, :] = o
            rows, q, v = load(n_chunks - 1 - i)
            o, sb = step(rows, q, v, sb, True)
            ob_ref[rows, :] = o
            return sf, sb

        zero = jnp.zeros((REC_DV, REC_DK), F32)
        lax.fori_loop(0, n_chunks, body, (zero, zero), unroll=math.gcd(n_chunks, unroll))

    @pl.when(bounded)
    def _():
        def step(rows, q, v, st, reverse):
            k_ref, w_ref = (kb_ref, wb_ref) if reverse else (kf_ref, wf_ref)
            return _hgrn_chunk_bounded(q, k_ref[0, rows, :], w_ref[rows, :], v, st,
                                       masks_ref.at[1 if reverse else 0], reverse)
        sweep(step, 6)

    @pl.when(jnp.logical_not(bounded))
    def _():
        def step(rows, q, v, st, reverse):
            k_ref, g_ref = (kb_ref, lfb_ref) if reverse else (kf_ref, lff_ref)
            return _hgrn_chunk_any(q, k_ref[0, rows, :], g_ref[0, rows, :], v, st,
                                   masks_ref.at[1 if reverse else 0], reverse)
        sweep(step, 2)

    def epilogue(t, carry):
        rows = pl.ds(pl.multiple_of(t * BLOCK, BLOCK), BLOCK)
        o = of_ref[rows, :] + ob_ref[rows, :]
        o = _rms(o, gain_ref[...]) * sg_ref[0, rows, :].astype(F32)
        o_ref[0, rows, :] = o.astype(BF16)
        return carry

    n_tiles = n_chunks * CHUNK // BLOCK
    lax.fori_loop(0, n_tiles, epilogue, 0, unroll=math.gcd(n_tiles, 3))


def _hgrn(qr, lff, kf, lfb, kb, ir, sg, gain_all, layer, masks, tri):
    ns, L, _ = qr.shape
    head = pl.BlockSpec((1, L, REC_DK), lambda s, hd: (s, 0, hd))
    scratch = pltpu.VMEM((L, REC_DV), F32)
    return pl.pallas_call(
        functools.partial(_hgrn_kernel, L // CHUNK),
        grid=(ns, REC_HEADS),
        in_specs=[head, head, head, head, head, head, head,
                  pl.BlockSpec((None, 1, REC_DV), lambda s, hd: (layer, 0, hd)),
                  pl.BlockSpec(masks.shape, lambda s, hd: (0, 0, 0, 0)),
                  pl.BlockSpec(tri.shape, lambda s, hd: (0, 0, 0))],
        out_specs=head,
        out_shape=jax.ShapeDtypeStruct((ns, L, REC_W), BF16),
        scratch_shapes=[scratch, scratch, scratch, scratch],
        compiler_params=pltpu.CompilerParams(
            dimension_semantics=("arbitrary", "arbitrary"), vmem_limit_bytes=VMEM_LIMIT),
        name="hgrn",
    )(qr, lff, kf, lfb, kb, ir, sg, gain_all, masks, tri)


def _prep_ffn(w_in, w_out):
    depth = w_in.shape[0]
    a = w_in[:, :, :D_FF].reshape(depth, D_MODEL, N_FF_CHUNKS, FF_CHUNK)
    b = w_in[:, :, D_FF:].reshape(depth, D_MODEL, N_FF_CHUNKS, FF_CHUNK)
    win_c = jnp.transpose(jnp.concatenate([a, b], axis=3), (0, 2, 1, 3)).astype(BF16)
    wout_c = w_out.reshape(depth, N_FF_CHUNKS, FF_CHUNK, D_MODEL).astype(BF16)
    return win_c, wout_c


def _prep_in(w):
    depth = w.shape[0]
    half = REC_W // 2
    q = w[:, :, :ATTN_W] * (HEAD_DIM ** -0.5 * LOG2E)
    k = w[:, :, ATTN_W:ATTN_W + KV_W]
    v = w[:, :, ATTN_W + KV_W:ATTN_W + 2 * KV_W]
    c0 = ATTN_W + 2 * KV_W
    qr, zf, zb, ir, gr = (w[:, :, c0 + i * REC_W:c0 + (i + 1) * REC_W] for i in range(5))

    def dup(t):
        t = t.reshape(depth, D_MODEL, N_KV, 1, HEAD_DIM)
        return jnp.broadcast_to(t, (depth, D_MODEL, N_KV, 2, HEAD_DIM)).reshape(depth, D_MODEL, 2 * KV_W)

    return jnp.concatenate([zf[..., :half], q[..., :half], zf[..., half:], q[..., half:], zb[..., :half], dup(k),
                            zb[..., half:], dup(v), qr, gr, ir], axis=2).astype(BF16)


def kernel(x_prompt, x_sample, meta_tokens, rel_bias, lb_logits, norm_gains, w_in, attn_sink, attn_out_gain,
           rec_out_gain, w_out, ffn1_w_in, ffn1_w_out, ffn2_w_in, ffn2_w_out):
    depth = w_in.shape[0]
    n_prompt = x_prompt.shape[0]
    ns = n_prompt + x_sample.shape[0]
    seq = x_prompt.shape[1]
    L = BLOCK + seq
    n_blocks = L // BLOCK
    bias = _attn_bias(rel_bias, n_blocks)
    masks_np, tri_np = _hgrn_consts()
    masks = jnp.asarray(masks_np)
    tri = jnp.asarray(tri_np, dtype=BF16)
    lbl = lb_logits.astype(F32)
    gains = norm_gains.astype(F32).reshape(depth * N_NORMS, 1, D_MODEL)
    ffn1 = _prep_ffn(ffn1_w_in, ffn1_w_out)
    ffn2 = _prep_ffn(ffn2_w_in, ffn2_w_out)
    w_in_all = _prep_in(w_in)
    w_out_all = w_out.astype(BF16)
    sink = attn_sink.astype(F32) * LOG2E
    attn_gain = attn_out_gain.astype(F32).reshape(depth, 1, ATTN_W)
    rec_gain = rec_out_gain.astype(F32).reshape(depth, 1, REC_W)
    for l in range(depth):
        if l == 0:
            h = _ffn_embed((x_prompt.astype(F32), x_sample.astype(F32)), meta_tokens, gains, *ffn1, L)
        else:
            h = _ffn(h, gains, 0, *ffn1, l)
        qa, k2, v2, qr, lff, kf, lfb, kb, ir, sg = _inproj(h, gains, 2, w_in_all, lbl, l)
        attn = _attention(qa, k2, v2, bias, sink, attn_gain, l)
        rec = _hgrn(qr, lff, kf, lfb, kb, ir, sg, rec_gain, l, masks, tri)
        mix = (attn, rec, w_out_all, 3)
        if l < depth - 1:
            h = _ffn(h, gains, 4, *ffn2, l, mix=mix)
    outs = [_ffn(h, gains, 4, *ffn2, depth - 1, mix=mix, final=part)
            for part in ((0, n_prompt), (n_prompt, ns - n_prompt))]
    return tuple(o.reshape(-1, seq, D_MODEL) for o in outs)
```
